```python
import math
import jax, jax.numpy as jnp
from jax import lax
import numpy as np

D_MODEL = 2048
BATCH = 1
SEQ = 8192
DEPTH = 4

N_META = 16
W_A = D_MODEL
H_A = 16
HD_A = W_A // H_A
CONV_A = 4
LRU_C = 8.0
W_B = D_MODEL
CONV_B = 31
EPS = 1e-6
SPLIT_SIZES = (W_A, W_A, W_B, W_B, W_B, D_MODEL, D_MODEL)
N_IN = sum(SPLIT_SIZES)
SPLIT_IDX = tuple(int(v) for v in np.cumsum(SPLIT_SIZES)[:-1])

kernel_name = "hybrid_rglru_conformer_conv_block"


def rms_norm(x, g):
    xf = x.astype(jnp.float32)
    y = xf * lax.rsqrt(jnp.mean(xf * xf, axis=-1, keepdims=True) + EPS)
    return (y * g.astype(jnp.float32)).astype(x.dtype)


def layer_norm(x, g, b):
    xf = x.astype(jnp.float32)
    mu = jnp.mean(xf, axis=-1, keepdims=True)
    xc = xf - mu
    var = jnp.mean(xc * xc, axis=-1, keepdims=True)
    y = xc * lax.rsqrt(var + EPS) * g.astype(jnp.float32) + b.astype(jnp.float32)
    return y.astype(x.dtype)


def causal_depthwise_conv(x, w, b):
    k, c = w.shape
    out = lax.conv_general_dilated(
        x, w[:, None, :].astype(x.dtype), window_strides=(1,), padding=[(k - 1, 0)],
        dimension_numbers=("NWC", "WIO", "NWC"), feature_group_count=c)
    return out + b.astype(x.dtype)


def rg_lru(x, w_r, b_r, w_i, b_i, lam):
    bsz, t, c = x.shape
    xh = x.reshape(bsz, t, H_A, HD_A)
    r = jax.nn.sigmoid(jnp.einsum("bthi,hij->bthj", xh, w_r).reshape(bsz, t, c) + b_r)
    i = jax.nn.sigmoid(jnp.einsum("bthi,hij->bthj", xh, w_i).reshape(bsz, t, c) + b_i)
    log_a = -LRU_C * r.astype(jnp.float32) * jax.nn.softplus(-lam.astype(jnp.float32))
    a = jnp.exp(log_a)
    mult = jnp.sqrt(jnp.maximum(-jnp.expm1(2.0 * log_a), 0.0))
    u = mult * (i * x).astype(jnp.float32)

    def combine(left, right):
        a1, b1 = left
        a2, b2 = right
        return a1 * a2, a2 * b1 + b2

    _, h = lax.associative_scan(combine, (a, u), axis=1)
    return h.astype(x.dtype)


def setup_inputs(seed: int = 0) -> dict:
    key = jax.random.key(seed)
    ks = jax.random.split(key, 24)
    f32 = jnp.float32
    nrm = lambda k, shape, s: (jax.random.normal(k, shape, f32) * s).astype(f32)
    x = jax.random.normal(ks[0], (BATCH, SEQ, D_MODEL), f32)
    meta = nrm(ks[1], (N_META, D_MODEL), 1.0)
    norm_g = 1.0 + nrm(ks[2], (DEPTH, D_MODEL), 0.05)
    w_in = nrm(ks[3], (DEPTH, D_MODEL, N_IN), D_MODEL ** -0.5)
    conv_a_w = nrm(ks[4], (DEPTH, CONV_A, W_A), CONV_A ** -0.5)
    conv_a_b = nrm(ks[5], (DEPTH, W_A), 0.02)
    w_rg = nrm(ks[6], (DEPTH, H_A, HD_A, HD_A), HD_A ** -0.5)
    b_rg = nrm(ks[7], (DEPTH, W_A), 0.02)
    w_ig = nrm(ks[8], (DEPTH, H_A, HD_A, HD_A), HD_A ** -0.5)
    b_ig = nrm(ks[9], (DEPTH, W_A), 0.02)
    a_c = jax.random.uniform(ks[10], (DEPTH, W_A), f32, 0.9, 0.999)
    s = a_c ** (1.0 / LRU_C)
    lru_lambda = jnp.log(s) - jnp.log1p(-s)
    conv_b_w = nrm(ks[11], (DEPTH, CONV_B, W_B), CONV_B ** -0.5)
    conv_b_b = nrm(ks[12], (DEPTH, W_B), 0.02)
    ln_b_g = 1.0 + nrm(ks[13], (DEPTH, W_B), 0.05)
    ln_b_b = nrm(ks[14], (DEPTH, W_B), 0.02)
    w_proj_a = nrm(ks[15], (DEPTH, W_A, D_MODEL), W_A ** -0.5)
    w_proj_b = nrm(ks[16], (DEPTH, W_B, D_MODEL), W_B ** -0.5)
    w_out = nrm(ks[17], (DEPTH, D_MODEL, D_MODEL), D_MODEL ** -0.5)
    final_g = 1.0 + nrm(ks[18], (D_MODEL,), 0.05)
    return {"x": x, "meta": meta, "norm_g": norm_g, "w_in": w_in,
            "conv_a_w": conv_a_w, "conv_a_b": conv_a_b, "w_rg": w_rg, "b_rg": b_rg,
            "w_ig": w_ig, "b_ig": b_ig, "lru_lambda": lru_lambda,
            "conv_b_w": conv_b_w, "conv_b_b": conv_b_b, "ln_b_g": ln_b_g, "ln_b_b": ln_b_b,
            "w_proj_a": w_proj_a, "w_proj_b": w_proj_b, "w_out": w_out, "final_g": final_g}


def reference(x, meta, norm_g, w_in, conv_a_w, conv_a_b, w_rg, b_rg, w_ig, b_ig, lru_lambda,
              conv_b_w, conv_b_b, ln_b_g, ln_b_b, w_proj_a, w_proj_b, w_out, final_g):
    bsz = x.shape[0]
    meta_b = jnp.broadcast_to(meta.astype(x.dtype)[None], (bsz, N_META, x.shape[-1]))
    h = jnp.concatenate([meta_b, x], axis=1)
    for l in range(DEPTH):
        hn = rms_norm(h, norm_g[l])
        z = jnp.einsum("btd,dn->btn", hn, w_in[l])
        xa, sa, vb, gb, sb, ma, mb = jnp.split(z, SPLIT_IDX, axis=-1)
        ya = causal_depthwise_conv(xa, conv_a_w[l], conv_a_b[l])
        ya = rg_lru(ya, w_rg[l], b_rg[l], w_ig[l], b_ig[l], lru_lambda[l])
        ya = ya * jax.nn.silu(sa)
        pa = jnp.einsum("btc,cd->btd", ya, w_proj_a[l])
        yb = vb * jax.nn.sigmoid(gb)
        yb = causal_depthwise_conv(yb, conv_b_w[l], conv_b_b[l])
        yb = jax.nn.silu(layer_norm(yb, ln_b_g[l], ln_b_b[l])) * jax.nn.silu(sb)
        pb = jnp.einsum("btc,cd->btd", yb, w_proj_b[l])
        merged = jax.nn.sigmoid(ma) * pa + jax.nn.sigmoid(mb) * pb
        h = h + jnp.einsum("btd,de->bte", merged, w_out[l])
    out = rms_norm(h, final_g)
    return out[:, N_META:, :]
```

```python
import functools

import jax
import jax.numpy as jnp
from jax import lax
from jax.experimental import pallas as pl
from jax.experimental.pallas import tpu as pltpu

F32 = jnp.float32
BF16 = jnp.bfloat16

EPS = 1e-6
LRU_C = 8.0
LANES = 128
NSEG = 48

ROWS_INPROJ = 912
COLS_INPROJ = 1024
COLS_GLU = 512
ROWS_OUT = 144
ROWS_NORM = 432
GATE_CHUNK_GROUPS = 9

MIB = 1024 * 1024


def _sigmoid(x):
    return 0.5 * jnp.tanh(0.5 * x) + 0.5


def _silu(x):
    return x * _sigmoid(x)


def _identity(x):
    return x


def _glu(v, g):
    return v * _sigmoid(g)


def _rmsnorm_kernel(x_ref, g_ref, o_ref):
    x = x_ref[...]
    ms = jnp.mean(x * x, axis=-1, keepdims=True)
    o_ref[...] = (x * lax.rsqrt(ms + EPS) * g_ref[...]).astype(o_ref.dtype)


def _rmsnorm(h, g):
    t, d = h.shape
    return pl.pallas_call(
        _rmsnorm_kernel,
        grid=(t // ROWS_NORM,),
        in_specs=[pl.BlockSpec((ROWS_NORM, d), lambda i: (i, 0)),
                  pl.BlockSpec((1, d), lambda i: (0, 0))],
        out_specs=pl.BlockSpec((ROWS_NORM, d), lambda i: (i, 0)),
        out_shape=jax.ShapeDtypeStruct((t, d), BF16),
        compiler_params=pltpu.CompilerParams(dimension_semantics=("parallel",)),
        name="rmsnorm0",
    )(h, g.reshape(1, d))


def _inproj_kernel(hn_ref, *refs, n_w, epilogue, slab_out):
    w_refs = refs[:n_w]
    o_ref = refs[n_w]
    wb_refs = refs[n_w + 1:]

    @pl.when(pl.program_id(1) == 0)
    def _cast_weights():
        for w_ref, wb_ref in zip(w_refs, wb_refs):
            wb_ref[...] = w_ref[...].astype(BF16)

    hn = hn_ref[...]
    zs = [jnp.dot(hn, wb_ref[...], preferred_element_type=F32) for wb_ref in wb_refs]
    y = epilogue(*zs).astype(o_ref.dtype)
    if slab_out:
        for s in range(o_ref.shape[0]):
            o_ref[s] = y[:, s * LANES:(s + 1) * LANES]
    else:
        o_ref[...] = y


def _inproj(hn, w_in, layer, col_starts, n_cols, epilogue, slab_out, name):
    t, d = hn.shape
    n_w = len(col_starts)
    tn = COLS_INPROJ if n_w == 1 else COLS_GLU
    tm = ROWS_INPROJ
    assert t % tm == 0 and n_cols % tn == 0 and all(c % tn == 0 for c in col_starts)
    grid = (n_cols // tn, t // tm)

    def w_spec(c0):
        return pl.BlockSpec((None, d, tn), lambda j, i: (layer, 0, c0 // tn + j))

    if slab_out:
        out_shape = jax.ShapeDtypeStruct((n_cols // LANES, t, LANES), BF16)
        out_spec = pl.BlockSpec((tn // LANES, tm, LANES), lambda j, i: (j, i, 0))
    else:
        out_shape = jax.ShapeDtypeStruct((t, n_cols), BF16)
        out_spec = pl.BlockSpec((tm, tn), lambda j, i: (i, j))

    vmem = (2 * tm * d * 2 + n_w * (2 * d * tn * 4 + d * tn * 2)
            + 2 * tm * tn * 2 + (n_w + 2) * tm * tn * 4 + 4 * MIB)
    return pl.pallas_call(
        functools.partial(_inproj_kernel, n_w=n_w, epilogue=epilogue, slab_out=slab_out),
        grid=grid,
        in_specs=[pl.BlockSpec((tm, d), lambda j, i: (i, 0))] + [w_spec(c) for c in col_starts],
        out_specs=out_spec,
        out_shape=out_shape,
        scratch_shapes=[pltpu.VMEM((d, tn), BF16) for _ in range(n_w)],
        compiler_params=pltpu.CompilerParams(
            dimension_semantics=("parallel", "arbitrary"), vmem_limit_bytes=vmem),
        name=name,
    )(hn, *([w_in] * n_w))


def _fill_conv_buffer(src_ref, xe_ref, halo, n_groups):
    def body(j, carry):
        r0 = pl.multiple_of(j * NSEG, NSEG)
        xe_ref[pl.ds(halo * NSEG + r0, NSEG), :] = src_ref[pl.ds(r0, NSEG), :].astype(F32)
        return carry

    lax.fori_loop(0, n_groups, body, 0)
    tail = xe_ref[pl.ds(n_groups * NSEG, halo * NSEG), :]
    rolled = pltpu.roll(tail, 1, 0).reshape(halo, NSEG, LANES)
    seg = lax.broadcasted_iota(jnp.int32, (halo, NSEG, LANES), 1)
    xe_ref[pl.ds(0, halo * NSEG), :] = jnp.where(seg == 0, 0.0, rolled).reshape(halo * NSEG, LANES)


def _mix_kernel(xa_ref, ssa_ref, glu_ref, caw_ref, cab_ref, wr_ref, wi_ref, br_ref, bi_ref,
                lam_ref, cbw_ref, cbb_ref, ya_ref, ybc_ref,
                xe_a, xe_b, abuf, ubuf, hend_ref, pend_ref, cstart_ref):
    t = xa_ref.shape[0]
    n_groups = t // NSEG
    k_a = caw_ref.shape[0]
    k_b = cbw_ref.shape[0]
    chunk = GATE_CHUNK_GROUPS * NSEG

    _fill_conv_buffer(xa_ref, xe_a, k_a - 1, n_groups)
    w_gate = jnp.concatenate([wr_ref[...], wi_ref[...]], axis=1).astype(BF16)
    neg_lam = -lam_ref[...]
    softplus = jnp.maximum(neg_lam, 0.0) + jnp.log1p(jnp.exp(-jnp.abs(neg_lam)))
    c_lru = -LRU_C * softplus
    caw = caw_ref[...]

    def gate_chunk(ci, carry):
        r0 = pl.multiple_of(ci * chunk, chunk)
        y = cab_ref[...] + caw[0:1] * xe_a[pl.ds(r0, chunk), :]
        for k in range(1, k_a):
            y = y + caw[k:k + 1] * xe_a[pl.ds(r0 + k * NSEG, chunk), :]
        g = jnp.dot(y.astype(BF16), w_gate, preferred_element_type=F32)
        r = _sigmoid(g[:, :LANES] + br_ref[...])
        i = _sigmoid(g[:, LANES:] + bi_ref[...])
        log_a = c_lru * r
        a = jnp.exp(log_a)
        mult = jnp.sqrt(jnp.maximum(-jnp.tanh(log_a) * (a * a + 1.0), 0.0))
        abuf[pl.ds(r0, chunk), :] = a
        ubuf[pl.ds(r0, chunk), :] = mult * (i * y)
        return carry

    lax.fori_loop(0, n_groups // GATE_CHUNK_GROUPS, gate_chunk, 0)

    def scan_step(j, carry):
        h, p = carry
        r0 = pl.multiple_of(j * NSEG, NSEG)
        a = abuf[pl.ds(r0, NSEG), :]
        h = a * h + ubuf[pl.ds(r0, NSEG), :]
        p = a * p
        ubuf[pl.ds(r0, NSEG), :] = h
        abuf[pl.ds(r0, NSEG), :] = p
        return h, p

    h_end, p_end = lax.fori_loop(
        0, n_groups, scan_step,
        (jnp.zeros((NSEG, LANES), F32), jnp.ones((NSEG, LANES), F32)))

    hend_ref[...] = h_end
    pend_ref[...] = p_end
    c = jnp.zeros((1, LANES), F32)
    for r in range(NSEG):
        cstart_ref[pl.ds(r, 1), :] = c
        c = hend_ref[pl.ds(r, 1), :] + pend_ref[pl.ds(r, 1), :] * c
    c_start = cstart_ref[...]

    def fix_step(j, carry):
        r0 = pl.multiple_of(j * NSEG, NSEG)
        h = ubuf[pl.ds(r0, NSEG), :] + abuf[pl.ds(r0, NSEG), :] * c_start
        ya_ref[pl.ds(r0, NSEG), :] = (h * ssa_ref[pl.ds(r0, NSEG), :].astype(F32)).astype(BF16)
        return carry

    lax.fori_loop(0, n_groups, fix_step, 0)

    _fill_conv_buffer(glu_ref, xe_b, k_b - 1, n_groups)
    cbw = cbw_ref[...]
    w_rows = [jnp.broadcast_to(cbw[k:k + 1], (NSEG, LANES)) for k in range(k_b)]
    bias_b = jnp.broadcast_to(cbb_ref[...], (NSEG, LANES))

    def conv_step(j, carry):
        r0 = pl.multiple_of(j * NSEG, NSEG)
        acc = bias_b
        for k in range(k_b):
            acc = acc + w_rows[k] * xe_b[pl.ds(r0 + k * NSEG, NSEG), :]
        ybc_ref[pl.ds(r0, NSEG), :] = acc.astype(BF16)
        return carry

    lax.fori_loop(0, n_groups, conv_step, 0)


def _mix(xa, ssa, glu, layer, conv_a_w, conv_a_b, w_rg, b_rg, w_ig, b_ig, lru_lambda,
         conv_b_w, conv_b_b):
    n_heads, t, _ = xa.shape
    k_a = conv_a_w.shape[1]
    k_b = conv_b_w.shape[1]
    n_groups = t // NSEG
    assert t % NSEG == 0 and n_groups % GATE_CHUNK_GROUPS == 0 and n_groups >= k_b - 1

    slab = pl.BlockSpec((None, t, LANES), lambda c: (c, 0, 0))

    def per_channel(k):
        return pl.BlockSpec((None, k, LANES), lambda c: (layer, 0, c))

    gate_w = pl.BlockSpec((None, None, LANES, LANES), lambda c: (layer, c, 0, 0))
    vmem = (3 * 2 * t * LANES * 2 + 2 * 2 * t * LANES * 2
            + (4 * t + (k_a + k_b - 2) * NSEG) * LANES * 4 + 8 * MIB)
    row3 = lambda p: p.reshape(p.shape[0], 1, p.shape[1])
    return pl.pallas_call(
        _mix_kernel,
        grid=(n_heads,),
        in_specs=[slab, slab, slab, per_channel(k_a), per_channel(1), gate_w, gate_w,
                  per_channel(1), per_channel(1), per_channel(1), per_channel(k_b), per_channel(1)],
        out_specs=[slab, slab],
        out_shape=[jax.ShapeDtypeStruct(xa.shape, BF16), jax.ShapeDtypeStruct(xa.shape, BF16)],
        scratch_shapes=[
            pltpu.VMEM(((k_a - 1 + n_groups) * NSEG, LANES), F32),
            pltpu.VMEM(((k_b - 1 + n_groups) * NSEG, LANES), F32),
            pltpu.VMEM((t, LANES), F32),
            pltpu.VMEM((t, LANES), F32),
            pltpu.VMEM((NSEG, LANES), F32),
            pltpu.VMEM((NSEG, LANES), F32),
            pltpu.VMEM((NSEG, LANES), F32),
        ],
        compiler_params=pltpu.CompilerParams(
            dimension_semantics=("parallel",), vmem_limit_bytes=vmem),
        name="time_mix",
    )(xa, ssa, glu, conv_a_w, row3(conv_a_b), w_rg, w_ig, row3(b_rg), row3(b_ig),
      row3(lru_lambda), conv_b_w, row3(conv_b_b))


def _out_kernel(ya_ref, ybc_ref, ssb_ref, gma_ref, gmb_ref, h_ref, wpa_ref, wpb_ref, wout_ref,
                lng_ref, lnb_ref, gn_ref, *out_refs, last):
    n_slabs = ya_ref.shape[0]
    ya = jnp.concatenate([ya_ref[s] for s in range(n_slabs)], axis=1)
    yc = jnp.concatenate([ybc_ref[s] for s in range(n_slabs)], axis=1).astype(F32)
    mu = jnp.mean(yc, axis=-1, keepdims=True)
    xc = yc - mu
    var = jnp.mean(xc * xc, axis=-1, keepdims=True)
    ln = xc * lax.rsqrt(var + EPS) * lng_ref[...] + lnb_ref[...]
    yb = (_silu(ln) * ssb_ref[...].astype(F32)).astype(BF16)
    pa = jnp.dot(ya, wpa_ref[...], preferred_element_type=F32)
    pb = jnp.dot(yb, wpb_ref[...], preferred_element_type=F32)
    merged = (gma_ref[...].astype(F32) * pa + gmb_ref[...].astype(F32) * pb).astype(BF16)
    h_new = h_ref[...] + jnp.dot(merged, wout_ref[...], preferred_element_type=F32)
    ms = jnp.mean(h_new * h_new, axis=-1, keepdims=True)
    hn = h_new * lax.rsqrt(ms + EPS) * gn_ref[...]
    if last:
        out_refs[0][...] = hn
    else:
        out_refs[0][...] = h_new
        out_refs[1][...] = hn.astype(BF16)


def _out_block(ya, ybc, ssb, gm, h, wpa, wpb, wout, layer, ln_g, ln_b, g_next, last):
    n_slabs, t, _ = ya.shape
    d = h.shape[1]
    tm = ROWS_OUT
    assert t % tm == 0
    slab = pl.BlockSpec((n_slabs, tm, LANES), lambda i: (0, i, 0))
    rows = pl.BlockSpec((tm, d), lambda i: (i, 0))
    weight = pl.BlockSpec((None, d, d), lambda i: (layer, 0, 0), pipeline_mode=pl.Buffered(1))
    vec = pl.BlockSpec((1, d), lambda i: (0, 0))
    if last:
        out_shape = [jax.ShapeDtypeStruct((t, d), F32)]
        out_specs = [rows]
    else:
        out_shape = [jax.ShapeDtypeStruct((t, d), F32), jax.ShapeDtypeStruct((t, d), BF16)]
        out_specs = [rows, rows]
    vmem = 3 * d * d * 2 + 2 * tm * d * (5 * 2 + 4 + 4 + 2) + 8 * tm * d * 4 + 4 * MIB
    return pl.pallas_call(
        functools.partial(_out_kernel, last=last),
        grid=(t // tm,),
        in_specs=[slab, slab, rows,
                  pl.BlockSpec((tm, d), lambda i: (i, 0)),
                  pl.BlockSpec((tm, d), lambda i: (i, 1)),
                  rows, weight, weight, weight, vec, vec, vec],
        out_specs=out_specs,
        out_shape=out_shape,
        compiler_params=pltpu.CompilerParams(
            dimension_semantics=("parallel",), vmem_limit_bytes=vmem),
        name="out_block",
    )(ya, ybc, ssb, gm, gm, h, wpa, wpb, wout, ln_g.reshape(1, d), ln_b.reshape(1, d),
      g_next.reshape(1, d))


def kernel(x, meta, norm_g, w_in, conv_a_w, conv_a_b, w_rg, b_rg, w_ig, b_ig, lru_lambda,
           conv_b_w, conv_b_b, ln_b_g, ln_b_b, w_proj_a, w_proj_b, w_out, final_g):
    bsz, seq, d = x.shape
    depth = w_in.shape[0]
    n_meta = meta.shape[0]
    t = n_meta + seq
    n_groups = t // NSEG
    assert t % NSEG == 0
    w_a = conv_a_w.shape[2]
    w_b = conv_b_w.shape[2]
    c_sa, c_vb, c_gb, c_sb, c_m = w_a, 2 * w_a, 2 * w_a + w_b, 2 * w_a + 2 * w_b, 2 * w_a + 3 * w_b

    wpa = w_proj_a.astype(BF16)
    wpb = w_proj_b.astype(BF16)
    wout = w_out.astype(BF16)

    outs = []
    for b in range(bsz):
        h = jnp.concatenate([meta.astype(x.dtype), x[b]], axis=0)
        h = h.reshape(NSEG, n_groups, d).transpose(1, 0, 2).reshape(t, d)
        hn = _rmsnorm(h, norm_g[0])
        for l in range(depth):
            xa = _inproj(hn, w_in, l, (0,), w_a, _identity, True, "inproj_xa")
            ssa = _inproj(hn, w_in, l, (c_sa,), w_a, _silu, True, "inproj_sa")
            glu = _inproj(hn, w_in, l, (c_vb, c_gb), w_b, _glu, True, "inproj_glu")
            ssb = _inproj(hn, w_in, l, (c_sb,), w_b, _silu, False, "inproj_sb")
            gm = _inproj(hn, w_in, l, (c_m,), 2 * d, _sigmoid, False, "inproj_merge_gates")
            ya, ybc = _mix(xa, ssa, glu, l, conv_a_w, conv_a_b, w_rg, b_rg, w_ig, b_ig,
                           lru_lambda, conv_b_w, conv_b_b)
            last = l == depth - 1
            g_next = final_g if last else norm_g[l + 1]
            res = _out_block(ya, ybc, ssb, gm, h, wpa, wpb, wout, l, ln_b_g[l], ln_b_b[l],
                             g_next, last)
            if last:
                out = res[0]
            else:
                h, hn = res
        out = out.reshape(n_groups, NSEG, d).transpose(1, 0, 2).reshape(t, d)
        outs.append(out[n_meta:])
    return jnp.stack(outs, axis=0)
```

```python
import functools

import jax
import jax.numpy as jnp
from jax import lax
from jax.experimental import pallas as pl
from jax.experimental.pallas import tpu as pltpu

F32 = jnp.float32
BF16 = jnp.bfloat16

EPS = 1e-6
LRU_C = 8.0
LANES = 128
NSEG = 48

ROWS_INPROJ = 912
COLS_INPROJ = 1024
COLS_GLU = 512
ROWS_OUT = 144
ROWS_NORM = 432
MIX_TILE_GROUPS = 19
MIX_DOT_PARTS = 3
MIX_CONV_CHAINS = 2

MIB = 1024 * 1024


def _sigmoid(x):
    return 0.5 * jnp.tanh(0.5 * x) + 0.5


def _silu(x):
    return x * _sigmoid(x)


def _identity(x):
    return x


def _glu(v, g):
    return v * _sigmoid(g)


def _rmsnorm_kernel(x_ref, g_ref, o_ref):
    x = x_ref[...]
    ms = jnp.mean(x * x, axis=-1, keepdims=True)
    o_ref[...] = (x * lax.rsqrt(ms + EPS) * g_ref[...]).astype(o_ref.dtype)


def _rmsnorm(h, g):
    t, d = h.shape
    return pl.pallas_call(
        _rmsnorm_kernel,
        grid=(t // ROWS_NORM,),
        in_specs=[pl.BlockSpec((ROWS_NORM, d), lambda i: (i, 0)),
                  pl.BlockSpec((1, d), lambda i: (0, 0))],
        out_specs=pl.BlockSpec((ROWS_NORM, d), lambda i: (i, 0)),
        out_shape=jax.ShapeDtypeStruct((t, d), BF16),
        compiler_params=pltpu.CompilerParams(dimension_semantics=("parallel",)),
        name="rmsnorm0",
    )(h, g.reshape(1, d))


def _inproj_kernel(hn_ref, *refs, n_w, epilogue):
    w_refs = refs[:n_w]
    o_ref = refs[n_w]
    wb_refs = refs[n_w + 1:]

    @pl.when(pl.program_id(1) == 0)
    def _cast_weights():
        for w_ref, wb_ref in zip(w_refs, wb_refs):
            wb_ref[...] = w_ref[...].astype(BF16)

    hn = hn_ref[...]
    zs = [jnp.dot(hn, wb_ref[...], preferred_element_type=F32) for wb_ref in wb_refs]
    y = epilogue(*zs).astype(o_ref.dtype)
    for s in range(o_ref.shape[0]):
        o_ref[s] = y[:, s * LANES:(s + 1) * LANES]


def _inproj(hn, w_in, layer, col_starts, n_cols, epilogue, name):
    t, d = hn.shape
    n_w = len(col_starts)
    tn = COLS_INPROJ if n_w == 1 else COLS_GLU
    tm = ROWS_INPROJ
    assert t % tm == 0 and n_cols % tn == 0 and all(c % tn == 0 for c in col_starts)
    grid = (n_cols // tn, t // tm)

    def w_spec(c0):
        return pl.BlockSpec((None, d, tn), lambda j, i: (layer, 0, c0 // tn + j))

    vmem = (2 * tm * d * 2 + n_w * (2 * d * tn * 4 + d * tn * 2)
            + 2 * tm * tn * 2 + (n_w + 2) * tm * tn * 4 + 4 * MIB)
    return pl.pallas_call(
        functools.partial(_inproj_kernel, n_w=n_w, epilogue=epilogue),
        grid=grid,
        in_specs=[pl.BlockSpec((tm, d), lambda j, i: (i, 0))] + [w_spec(c) for c in col_starts],
        out_specs=pl.BlockSpec((tn // LANES, tm, LANES), lambda j, i: (j, i, 0)),
        out_shape=jax.ShapeDtypeStruct((n_cols // LANES, t, LANES), BF16),
        scratch_shapes=[pltpu.VMEM((d, tn), BF16) for _ in range(n_w)],
        compiler_params=pltpu.CompilerParams(
            dimension_semantics=("parallel", "arbitrary"), vmem_limit_bytes=vmem),
        name=name,
    )(hn, *([w_in] * n_w))


def _left_halo(src_ref, halo, n_groups):
    tail = src_ref[pl.ds((n_groups - halo) * NSEG, halo * NSEG), :].astype(F32)
    rolled = pltpu.roll(tail, 1, 0).reshape(halo, NSEG, LANES)
    seg = lax.broadcasted_iota(jnp.int32, (halo, NSEG, LANES), 1)
    return jnp.where(seg == 0, 0.0, rolled).reshape(halo * NSEG, LANES)


def _zero_after(*arrays):
    words = []
    for a in arrays:
        bits = lax.bitcast_convert_type(a, jnp.uint32)
        words += [bits[i:i + 8] for i in range(0, a.shape[0], 8)]
    word = functools.reduce(jnp.bitwise_or, words)
    word = lax.shift_right_logical(lax.shift_right_logical(word, jnp.uint32(16)), jnp.uint32(16))
    return lax.bitcast_convert_type(word, F32)


def _order_after(lhs, anchors):
    if not anchors:
        return lhs
    zero = _zero_after(*anchors)
    zero = jnp.concatenate([zero, zero], axis=0).astype(lhs.dtype)
    head = lhs[:16] + jnp.tile(zero, (1, lhs.shape[1] // LANES))
    return jnp.concatenate([head, lhs[16:]], axis=0)


def _mixproj_kernel(hn_ref, wsa_ref, wsb_ref, wma_ref, wmb_ref, xa_ref, glu_ref,
                    caw_ref, cab_ref, wr_ref, wi_ref, br_ref, bi_ref, lam_ref, cbw_ref, cbb_ref,
                    ssb_ref, gma_ref, gmb_ref, ybc_ref, ya_ref,
                    wb_ref, xa_halo, xe_b, hloc, pcum, ssa_buf, h_state, p_state, cstart_ref):
    rt = pl.program_id(1)
    n_rt = pl.num_programs(1)
    t = xa_ref.shape[0]
    n_groups = t // NSEG
    k_a = caw_ref.shape[0]
    k_b = cbw_ref.shape[0]
    tile = MIX_TILE_GROUPS * NSEG
    r0 = pl.multiple_of(rt * tile, tile)

    @pl.when(rt == 0)
    def _head_start():
        for s, w_ref in enumerate((wsa_ref, wsb_ref, wma_ref, wmb_ref)):
            wb_ref[:, s * LANES:(s + 1) * LANES] = w_ref[...].astype(BF16)
        xe_b[pl.ds(0, (k_b - 1) * NSEG), :] = _left_halo(glu_ref, k_b - 1, n_groups)

    xe_b[pl.ds((k_b - 1) * NSEG + r0, tile), :] = glu_ref[pl.ds(r0, tile), :].astype(F32)

    w_gate = jnp.concatenate([wr_ref[...], wi_ref[...]], axis=1).astype(BF16)
    neg_lam = -lam_ref[...]
    softplus = jnp.maximum(neg_lam, 0.0) + jnp.log1p(jnp.exp(-jnp.abs(neg_lam)))
    c_lru = -LRU_C * softplus
    caw = caw_ref[...]
    halo_a = (k_a - 1) * NSEG
    xa_tile = xa_ref[pl.ds(r0, tile), :].astype(F32)
    xa_prev = jnp.where(rt == 0, _left_halo(xa_ref, k_a - 1, n_groups), xa_halo[...])
    xa_halo[...] = xa_tile[tile - halo_a:]
    xa_in = jnp.concatenate([xa_prev, xa_tile], axis=0)
    y = cab_ref[...] + caw[0:1] * xa_in[:tile]
    for k in range(1, k_a):
        y = y + caw[k:k + 1] * xa_in[k * NSEG:k * NSEG + tile]
    g = jnp.dot(y.astype(BF16), w_gate, preferred_element_type=F32)

    h = jnp.where(rt == 0, 0.0, h_state[...])
    p = jnp.where(rt == 0, 1.0, p_state[...])
    b_r = jnp.broadcast_to(br_ref[...], (8, LANES))
    for jj in range(MIX_TILE_GROUPS):
        rows = slice(jj * NSEG, (jj + 1) * NSEG)
        r = _sigmoid(g[rows, :LANES] + jnp.tile(b_r, (NSEG // 8, 1)))
        i = _sigmoid(g[rows, LANES:] + bi_ref[...])
        log_a = c_lru * r
        a_j = jnp.exp(log_a)
        mult = jnp.sqrt(jnp.maximum(-jnp.tanh(log_a) * (a_j * a_j + 1.0), 0.0))
        h = a_j * h + mult * (i * y[rows])
        p = a_j * p
        hloc[pl.ds(r0 + jj * NSEG, NSEG), :] = h
        pcum[pl.ds(r0 + jj * NSEG, NSEG), :] = p
        b_r = br_ref[...] + _zero_after(h)
    h_state[...] = h
    p_state[...] = p

    cbw = cbw_ref[...]
    w_rows = [jnp.broadcast_to(cbw[k:k + 1], (8, LANES)) for k in range(k_b)]
    bias_b = jnp.broadcast_to(cbb_ref[...], (8, LANES))
    chain_tail = [None] * MIX_CONV_CHAINS
    for jj in range(MIX_TILE_GROUPS):
        accs = []
        for s in range(NSEG // 8):
            lane = (jj * (NSEG // 8) + s) % MIX_CONV_CHAINS
            acc = bias_b if chain_tail[lane] is None else bias_b + _zero_after(chain_tail[lane])
            for k in range(k_b):
                acc = acc + w_rows[k] * xe_b[pl.ds(r0 + (jj + k) * NSEG + s * 8, 8), :]
            chain_tail[lane] = acc
            accs.append(acc)
        ybc_ref[jj * NSEG:(jj + 1) * NSEG, :] = jnp.concatenate(accs, axis=0).astype(BF16)

    part = tile // MIX_DOT_PARTS
    n_pieces = 2 * MIX_DOT_PARTS
    anchors = [[] for _ in range(n_pieces)]
    piece = 0
    for m0 in range(0, tile, part):
        for half, (act, dsts) in enumerate((
                (_silu, ((ssa_buf, r0 + m0), (ssb_ref, m0))),
                (_sigmoid, ((gma_ref, m0), (gmb_ref, m0))))):
            hn = _order_after(hn_ref[m0:m0 + part, :], anchors[piece])
            z = jnp.dot(hn, wb_ref[:, half * 2 * LANES:(half + 1) * 2 * LANES],
                        preferred_element_type=F32)
            for s, (dst, row) in enumerate(dsts):
                out = act(z[:, s * LANES:(s + 1) * LANES])
                dst[pl.ds(row, part), :] = out.astype(BF16)
                if piece + 2 < n_pieces:
                    anchors[piece + 2].append(out)
            piece += 1

    @pl.when(rt == n_rt - 1)
    def _head_end():
        c = jnp.zeros((1, LANES), F32)
        for seg in range(NSEG):
            cstart_ref[pl.ds(seg, 1), :] = c
            c = h_state[pl.ds(seg, 1), :] + p_state[pl.ds(seg, 1), :] * c
        c_start = cstart_ref[...]

        def fix_step(j, carry):
            q0 = pl.multiple_of(j * NSEG, NSEG)
            hh = hloc[pl.ds(q0, NSEG), :] + pcum[pl.ds(q0, NSEG), :] * c_start
            ya_ref[pl.ds(q0, NSEG), :] = (hh * ssa_buf[pl.ds(q0, NSEG), :].astype(F32)).astype(BF16)
            return carry

        lax.fori_loop(0, n_groups, fix_step, 0)


def _mixproj(hn, w_in, xa, glu, layer, gate_cols, conv_a_w, conv_a_b, w_rg, b_rg, w_ig, b_ig,
             lru_lambda, conv_b_w, conv_b_b):
    n_heads, t, _ = xa.shape
    d = hn.shape[1]
    k_a = conv_a_w.shape[1]
    k_b = conv_b_w.shape[1]
    n_groups = t // NSEG
    tile = MIX_TILE_GROUPS * NSEG
    assert t % NSEG == 0 and n_groups % MIX_TILE_GROUPS == 0 and n_groups >= k_b - 1
    assert all(c % LANES == 0 for c in gate_cols)

    def w_strip(c0):
        return pl.BlockSpec((None, d, LANES), lambda c, i: (layer, 0, c0 // LANES + c))

    slab = pl.BlockSpec((None, t, LANES), lambda c, i: (c, 0, 0))
    slab_tile = pl.BlockSpec((None, tile, LANES), lambda c, i: (c, i, 0))

    def per_channel(k):
        return pl.BlockSpec((None, k, LANES), lambda c, i: (layer, 0, c))

    gate_w = pl.BlockSpec((None, None, LANES, LANES), lambda c, i: (layer, c, 0, 0))
    slab_shape = jax.ShapeDtypeStruct(xa.shape, BF16)
    row3 = lambda p: p.reshape(p.shape[0], 1, p.shape[1])
    vmem = (2 * tile * d * 2 + 4 * 2 * d * LANES * 4 + d * 4 * LANES * 2
            + 2 * 2 * t * LANES * 2 + (4 * t + (k_a + k_b - 2) * NSEG) * LANES * 4
            + t * LANES * 2 + 2 * t * LANES * 2 + 4 * 2 * tile * LANES * 2
            + 10 * tile * LANES * 4 + 4 * MIB)
    return pl.pallas_call(
        _mixproj_kernel,
        grid=(n_heads, n_groups // MIX_TILE_GROUPS),
        in_specs=[pl.BlockSpec((tile, d), lambda c, i: (i, 0))]
                 + [w_strip(c0) for c0 in gate_cols]
                 + [slab, slab, per_channel(k_a), per_channel(1), gate_w, gate_w,
                    per_channel(1), per_channel(1), per_channel(1), per_channel(k_b),
                    per_channel(1)],
        out_specs=[slab_tile, slab_tile, slab_tile, slab_tile, slab],
        out_shape=[slab_shape] * 5,
        scratch_shapes=[
            pltpu.VMEM((d, 4 * LANES), BF16),
            pltpu.VMEM(((k_a - 1) * NSEG, LANES), F32),
            pltpu.VMEM(((k_b - 1 + n_groups) * NSEG, LANES), F32),
            pltpu.VMEM((t, LANES), F32),
            pltpu.VMEM((t, LANES), F32),
            pltpu.VMEM((t, LANES), BF16),
            pltpu.VMEM((NSEG, LANES), F32),
            pltpu.VMEM((NSEG, LANES), F32),
            pltpu.VMEM((NSEG, LANES), F32),
        ],
        compiler_params=pltpu.CompilerParams(
            dimension_semantics=("parallel", "arbitrary"), vmem_limit_bytes=vmem),
        name="mixproj",
    )(hn, w_in, w_in, w_in, w_in, xa, glu, conv_a_w, row3(conv_a_b), w_rg, w_ig, row3(b_rg),
      row3(b_ig), row3(lru_lambda), conv_b_w, row3(conv_b_b))


def _from_slabs(ref):
    return jnp.concatenate([ref[s] for s in range(ref.shape[0])], axis=1)


def _out_kernel(ya_ref, ybc_ref, ssb_ref, gma_ref, gmb_ref, h_ref, wpa_ref, wpb_ref, wout_ref,
                lng_ref, lnb_ref, gn_ref, *out_refs, last):
    yc = _from_slabs(ybc_ref).astype(F32)
    mu = jnp.mean(yc, axis=-1, keepdims=True)
    xc = yc - mu
    var = jnp.mean(xc * xc, axis=-1, keepdims=True)
    ln = xc * lax.rsqrt(var + EPS) * lng_ref[...] + lnb_ref[...]
    yb = (_silu(ln) * _from_slabs(ssb_ref).astype(F32)).astype(BF16)
    pa = jnp.dot(_from_slabs(ya_ref), wpa_ref[...], preferred_element_type=F32)
    pb = jnp.dot(yb, wpb_ref[...], preferred_element_type=F32)
    merged = (_from_slabs(gma_ref).astype(F32) * pa
              + _from_slabs(gmb_ref).astype(F32) * pb).astype(BF16)
    h_new = h_ref[...] + jnp.dot(merged, wout_ref[...], preferred_element_type=F32)
    ms = jnp.mean(h_new * h_new, axis=-1, keepdims=True)
    hn = h_new * lax.rsqrt(ms + EPS) * gn_ref[...]
    if last:
        out_refs[0][...] = hn
    else:
        out_refs[0][...] = h_new
        out_refs[1][...] = hn.astype(BF16)


def _out_block(ya, ybc, ssb, gma, gmb, h, wpa, wpb, wout, layer, ln_g, ln_b, g_next, last):
    n_slabs, t, _ = ya.shape
    d = h.shape[1]
    tm = ROWS_OUT
    assert t % tm == 0
    slab = pl.BlockSpec((n_slabs, tm, LANES), lambda i: (0, i, 0))
    rows = pl.BlockSpec((tm, d), lambda i: (i, 0))
    weight = pl.BlockSpec((None, d, d), lambda i: (layer, 0, 0), pipeline_mode=pl.Buffered(1))
    vec = pl.BlockSpec((1, d), lambda i: (0, 0))
    if last:
        out_shape = [jax.ShapeDtypeStruct((t, d), F32)]
        out_specs = [rows]
    else:
        out_shape = [jax.ShapeDtypeStruct((t, d), F32), jax.ShapeDtypeStruct((t, d), BF16)]
        out_specs = [rows, rows]
    vmem = 3 * d * d * 2 + 2 * tm * d * (5 * 2 + 4 + 4 + 2) + 8 * tm * d * 4 + 4 * MIB
    return pl.pallas_call(
        functools.partial(_out_kernel, last=last),
        grid=(t // tm,),
        in_specs=[slab, slab, slab, slab, slab, rows, weight, weight, weight, vec, vec, vec],
        out_specs=out_specs,
        out_shape=out_shape,
        compiler_params=pltpu.CompilerParams(
            dimension_semantics=("parallel",), vmem_limit_bytes=vmem),
        name="out_block",
    )(ya, ybc, ssb, gma, gmb, h, wpa, wpb, wout, ln_g.reshape(1, d), ln_b.reshape(1, d),
      g_next.reshape(1, d))


def kernel(x, meta, norm_g, w_in, conv_a_w, conv_a_b, w_rg, b_rg, w_ig, b_ig, lru_lambda,
           conv_b_w, conv_b_b, ln_b_g, ln_b_b, w_proj_a, w_proj_b, w_out, final_g):
    bsz, seq, d = x.shape
    depth = w_in.shape[0]
    n_meta = meta.shape[0]
    t = n_meta + seq
    n_groups = t // NSEG
    assert t % NSEG == 0
    w_a = conv_a_w.shape[2]
    w_b = conv_b_w.shape[2]
    assert w_a == d and w_b == d
    c_sa, c_vb, c_gb, c_sb = w_a, 2 * w_a, 2 * w_a + w_b, 2 * w_a + 2 * w_b
    c_ma, c_mb = 2 * w_a + 3 * w_b, 2 * w_a + 3 * w_b + d

    wpa = w_proj_a.astype(BF16)
    wpb = w_proj_b.astype(BF16)
    wout = w_out.astype(BF16)

    outs = []
    for b in range(bsz):
        h = jnp.concatenate([meta.astype(x.dtype), x[b]], axis=0)
        h = h.reshape(NSEG, n_groups, d).transpose(1, 0, 2).reshape(t, d)
        hn = _rmsnorm(h, norm_g[0])
        for l in range(depth):
            xa = _inproj(hn, w_in, l, (0,), w_a, _identity, "inproj_xa")
            glu = _inproj(hn, w_in, l, (c_vb, c_gb), w_b, _glu, "inproj_glu")
            ssb, gma, gmb, ybc, ya = _mixproj(
                hn, w_in, xa, glu, l, (c_sa, c_sb, c_ma, c_mb), conv_a_w, conv_a_b,
                w_rg, b_rg, w_ig, b_ig, lru_lambda, conv_b_w, conv_b_b)
            last = l == depth - 1
            g_next = final_g if last else norm_g[l + 1]
            res = _out_block(ya, ybc, ssb, gma, gmb, h, wpa, wpb, wout, l, ln_b_g[l],
                             ln_b_b[l], g_next, last)
            if last:
                out = res[0]
            else:
                h, hn = res
        out = out.reshape(n_groups, NSEG, d).transpose(1, 0, 2).reshape(t, d)
        outs.append(out[n_meta:])
    return jnp.stack(outs, axis=0)
```

```python
import functools

import jax
import jax.numpy as jnp
from jax import lax
from jax.experimental import pallas as pl
from jax.experimental.pallas import tpu as pltpu

F32 = jnp.float32
BF16 = jnp.bfloat16

EPS = 1e-6
LRU_C = 8.0
LANES = 128
NSEG = 48

ROWS_INPROJ = 912
COLS_INPROJ = 1024
COLS_GLU = 512
ROWS_OUT = 144
ROWS_NORM = 432
MIX_TILE_GROUPS = 19
MIX_DOT_PARTS = 1
MIX_CONV_CHAINS = 2

MIB = 1024 * 1024


def _sigmoid(x):
    return 0.5 * jnp.tanh(0.5 * x) + 0.5


def _silu(x):
    return x * _sigmoid(x)


def _identity(x):
    return x


def _glu(v, g):
    return v * _sigmoid(g)


def _rmsnorm_kernel(x_ref, g_ref, o_ref):
    x = x_ref[...]
    ms = jnp.mean(x * x, axis=-1, keepdims=True)
    o_ref[...] = (x * lax.rsqrt(ms + EPS) * g_ref[...]).astype(o_ref.dtype)


def _rmsnorm(h, g):
    t, d = h.shape
    return pl.pallas_call(
        _rmsnorm_kernel,
        grid=(t // ROWS_NORM,),
        in_specs=[pl.BlockSpec((ROWS_NORM, d), lambda i: (i, 0)),
                  pl.BlockSpec((1, d), lambda i: (0, 0))],
        out_specs=pl.BlockSpec((ROWS_NORM, d), lambda i: (i, 0)),
        out_shape=jax.ShapeDtypeStruct((t, d), BF16),
        compiler_params=pltpu.CompilerParams(dimension_semantics=("parallel",)),
        name="rmsnorm0",
    )(h, g.reshape(1, d))


def _inproj_kernel(hn_ref, *refs, n_w, epilogue):
    w_refs = refs[:n_w]
    o_ref = refs[n_w]
    wb_refs = refs[n_w + 1:]

    @pl.when(pl.program_id(1) == 0)
    def _cast_weights():
        for w_ref, wb_ref in zip(w_refs, wb_refs):
            wb_ref[...] = w_ref[...].astype(BF16)

    hn = hn_ref[...]
    zs = [jnp.dot(hn, wb_ref[...], preferred_element_type=F32) for wb_ref in wb_refs]
    y = epilogue(*zs).astype(o_ref.dtype)
    for s in range(o_ref.shape[0]):
        o_ref[s] = y[:, s * LANES:(s + 1) * LANES]


def _inproj(hn, w_in, layer, col_starts, n_cols, epilogue, name):
    t, d = hn.shape
    n_w = len(col_starts)
    tn = COLS_INPROJ if n_w == 1 else COLS_GLU
    tm = ROWS_INPROJ
    assert t % tm == 0 and n_cols % tn == 0 and all(c % tn == 0 for c in col_starts)
    grid = (n_cols // tn, t // tm)

    def w_spec(c0):
        return pl.BlockSpec((None, d, tn), lambda j, i: (layer, 0, c0 // tn + j))

    vmem = (2 * tm * d * 2 + n_w * (2 * d * tn * 4 + d * tn * 2)
            + 2 * tm * tn * 2 + (n_w + 2) * tm * tn * 4 + 4 * MIB)
    return pl.pallas_call(
        functools.partial(_inproj_kernel, n_w=n_w, epilogue=epilogue),
        grid=grid,
        in_specs=[pl.BlockSpec((tm, d), lambda j, i: (i, 0))] + [w_spec(c) for c in col_starts],
        out_specs=pl.BlockSpec((tn // LANES, tm, LANES), lambda j, i: (j, i, 0)),
        out_shape=jax.ShapeDtypeStruct((n_cols // LANES, t, LANES), BF16),
        scratch_shapes=[pltpu.VMEM((d, tn), BF16) for _ in range(n_w)],
        compiler_params=pltpu.CompilerParams(
            dimension_semantics=("parallel", "arbitrary"), vmem_limit_bytes=vmem),
        name=name,
    )(hn, *([w_in] * n_w))


def _left_halo(src_ref, halo, n_groups):
    tail = src_ref[pl.ds((n_groups - halo) * NSEG, halo * NSEG), :].astype(F32)
    rolled = pltpu.roll(tail, 1, 0).reshape(halo, NSEG, LANES)
    seg = lax.broadcasted_iota(jnp.int32, (halo, NSEG, LANES), 1)
    return jnp.where(seg == 0, 0.0, rolled).reshape(halo * NSEG, LANES)


def _zero_after(*arrays):
    words = []
    for a in arrays:
        bits = lax.bitcast_convert_type(a, jnp.uint32)
        words += [bits[i:i + 8] for i in range(0, a.shape[0], 8)]
    word = functools.reduce(jnp.bitwise_or, words)
    word = lax.shift_right_logical(lax.shift_right_logical(word, jnp.uint32(16)), jnp.uint32(16))
    return lax.bitcast_convert_type(word, F32)


def _order_after(lhs, anchors):
    if not anchors:
        return lhs
    zero = _zero_after(*anchors)
    zero = jnp.concatenate([zero, zero], axis=0).astype(lhs.dtype)
    head = lhs[:16] + jnp.tile(zero, (1, lhs.shape[1] // LANES))
    return jnp.concatenate([head, lhs[16:]], axis=0)


def _mixproj_kernel(hn_ref, wsa_ref, wsb_ref, wma_ref, wmb_ref, xa_ref, glu_ref,
                    caw_ref, cab_ref, wr_ref, wi_ref, br_ref, bi_ref, lam_ref, cbw_ref, cbb_ref,
                    ssb_ref, gma_ref, gmb_ref, ybc_ref, ya_ref,
                    wb_ref, xa_halo, xe_b, hloc, pcum, ssa_buf, h_state, p_state, cstart_ref):
    rt = pl.program_id(1)
    n_rt = pl.num_programs(1)
    t = xa_ref.shape[0]
    n_groups = t // NSEG
    k_a = caw_ref.shape[0]
    k_b = cbw_ref.shape[0]
    tile = MIX_TILE_GROUPS * NSEG
    r0 = pl.multiple_of(rt * tile, tile)

    @pl.when(rt == 0)
    def _head_start():
        for s, w_ref in enumerate((wsa_ref, wsb_ref, wma_ref, wmb_ref)):
            wb_ref[:, s * LANES:(s + 1) * LANES] = w_ref[...].astype(BF16)
        xe_b[pl.ds(0, (k_b - 1) * NSEG), :] = _left_halo(glu_ref, k_b - 1, n_groups)

    xe_b[pl.ds((k_b - 1) * NSEG + r0, tile), :] = glu_ref[pl.ds(r0, tile), :].astype(F32)

    w_gate = jnp.concatenate([wr_ref[...], wi_ref[...]], axis=1).astype(BF16)
    neg_lam = -lam_ref[...]
    softplus = jnp.maximum(neg_lam, 0.0) + jnp.log1p(jnp.exp(-jnp.abs(neg_lam)))
    c_lru = -LRU_C * softplus
    caw = caw_ref[...]
    halo_a = (k_a - 1) * NSEG
    xa_tile = xa_ref[pl.ds(r0, tile), :].astype(F32)
    xa_prev = jnp.where(rt == 0, _left_halo(xa_ref, k_a - 1, n_groups), xa_halo[...])
    xa_halo[...] = xa_tile[tile - halo_a:]
    xa_in = jnp.concatenate([xa_prev, xa_tile], axis=0)
    y = cab_ref[...] + caw[0:1] * xa_in[:tile]
    for k in range(1, k_a):
        y = y + caw[k:k + 1] * xa_in[k * NSEG:k * NSEG + tile]
    g = jnp.dot(y.astype(BF16), w_gate, preferred_element_type=F32)

    h = jnp.where(rt == 0, 0.0, h_state[...])
    p = jnp.where(rt == 0, 1.0, p_state[...])
    b_r = jnp.broadcast_to(br_ref[...], (8, LANES))
    for jj in range(MIX_TILE_GROUPS):
        rows = slice(jj * NSEG, (jj + 1) * NSEG)
        r = _sigmoid(g[rows, :LANES] + jnp.tile(b_r, (NSEG // 8, 1)))
        i = _sigmoid(g[rows, LANES:] + bi_ref[...])
        log_a = c_lru * r
        a_j = jnp.exp(log_a)
        mult = jnp.sqrt(jnp.maximum(-jnp.tanh(log_a) * (a_j * a_j + 1.0), 0.0))
        h = a_j * h + mult * (i * y[rows])
        p = a_j * p
        hloc[pl.ds(r0 + jj * NSEG, NSEG), :] = h
        pcum[pl.ds(r0 + jj * NSEG, NSEG), :] = p
        b_r = br_ref[...] + _zero_after(h)
    h_state[...] = h
    p_state[...] = p

    cbw = cbw_ref[...]
    w_rows = [jnp.broadcast_to(cbw[k:k + 1], (8, LANES)) for k in range(k_b)]
    bias_b = jnp.broadcast_to(cbb_ref[...], (8, LANES))
    chain_tail = [None] * MIX_CONV_CHAINS
    for jj in range(MIX_TILE_GROUPS):
        accs = []
        for s in range(NSEG // 8):
            lane = (jj * (NSEG // 8) + s) % MIX_CONV_CHAINS
            acc = bias_b if chain_tail[lane] is None else bias_b + _zero_after(chain_tail[lane])
            for k in range(k_b):
                acc = acc + w_rows[k] * xe_b[pl.ds(r0 + (jj + k) * NSEG + s * 8, 8), :]
            chain_tail[lane] = acc
            accs.append(acc)
        ybc_ref[jj * NSEG:(jj + 1) * NSEG, :] = jnp.concatenate(accs, axis=0).astype(BF16)

    part = tile // MIX_DOT_PARTS
    n_pieces = 2 * MIX_DOT_PARTS
    anchors = [[] for _ in range(n_pieces)]
    piece = 0
    for m0 in range(0, tile, part):
        for half, (act, dsts) in enumerate((
                (_silu, ((ssa_buf, r0 + m0), (ssb_ref, m0))),
                (_sigmoid, ((gma_ref, m0), (gmb_ref, m0))))):
            hn = _order_after(hn_ref[m0:m0 + part, :], anchors[piece])
            z = jnp.dot(hn, wb_ref[:, half * 2 * LANES:(half + 1) * 2 * LANES],
                        preferred_element_type=F32)
            for s, (dst, row) in enumerate(dsts):
                out = act(z[:, s * LANES:(s + 1) * LANES])
                dst[pl.ds(row, part), :] = out.astype(BF16)
                if piece + 2 < n_pieces:
                    anchors[piece + 2].append(out)
            piece += 1

    @pl.when(rt == n_rt - 1)
    def _head_end():
        c = jnp.zeros((1, LANES), F32)
        for seg in range(NSEG):
            cstart_ref[pl.ds(seg, 1), :] = c
            c = h_state[pl.ds(seg, 1), :] + p_state[pl.ds(seg, 1), :] * c
        c_start = cstart_ref[...]

        def fix_step(j, carry):
            q0 = pl.multiple_of(j * NSEG, NSEG)
            hh = hloc[pl.ds(q0, NSEG), :] + pcum[pl.ds(q0, NSEG), :] * c_start
            ya_ref[pl.ds(q0, NSEG), :] = (hh * ssa_buf[pl.ds(q0, NSEG), :].astype(F32)).astype(BF16)
            return carry

        lax.fori_loop(0, n_groups, fix_step, 0)


def _mixproj(hn, w_in, xa, glu, layer, gate_cols, conv_a_w, conv_a_b, w_rg, b_rg, w_ig, b_ig,
             lru_lambda, conv_b_w, conv_b_b):
    n_heads, t, _ = xa.shape
    d = hn.shape[1]
    k_a = conv_a_w.shape[1]
    k_b = conv_b_w.shape[1]
    n_groups = t // NSEG
    tile = MIX_TILE_GROUPS * NSEG
    assert t % NSEG == 0 and n_groups % MIX_TILE_GROUPS == 0 and n_groups >= k_b - 1
    assert all(c % LANES == 0 for c in gate_cols)

    def w_strip(c0):
        return pl.BlockSpec((None, d, LANES), lambda c, i: (layer, 0, c0 // LANES + c))

    slab = pl.BlockSpec((None, t, LANES), lambda c, i: (c, 0, 0))
    slab_tile = pl.BlockSpec((None, tile, LANES), lambda c, i: (c, i, 0))

    def per_channel(k):
        return pl.BlockSpec((None, k, LANES), lambda c, i: (layer, 0, c))

    gate_w = pl.BlockSpec((None, None, LANES, LANES), lambda c, i: (layer, c, 0, 0))
    slab_shape = jax.ShapeDtypeStruct(xa.shape, BF16)
    row3 = lambda p: p.reshape(p.shape[0], 1, p.shape[1])
    vmem = (2 * tile * d * 2 + 4 * 2 * d * LANES * 4 + d * 4 * LANES * 2
            + 2 * 2 * t * LANES * 2 + (4 * t + (k_a + k_b - 2) * NSEG) * LANES * 4
            + t * LANES * 2 + 2 * t * LANES * 2 + 4 * 2 * tile * LANES * 2
            + 10 * tile * LANES * 4 + 4 * MIB)
    return pl.pallas_call(
        _mixproj_kernel,
        grid=(n_heads, n_groups // MIX_TILE_GROUPS),
        in_specs=[pl.BlockSpec((tile, d), lambda c, i: (i, 0))]
                 + [w_strip(c0) for c0 in gate_cols]
                 + [slab, slab, per_channel(k_a), per_channel(1), gate_w, gate_w,
                    per_channel(1), per_channel(1), per_channel(1), per_channel(k_b),
                    per_channel(1)],
        out_specs=[slab_tile, slab_tile, slab_tile, slab_tile, slab],
        out_shape=[slab_shape] * 5,
        scratch_shapes=[
            pltpu.VMEM((d, 4 * LANES), BF16),
            pltpu.VMEM(((k_a - 1) * NSEG, LANES), F32),
            pltpu.VMEM(((k_b - 1 + n_groups) * NSEG, LANES), F32),
            pltpu.VMEM((t, LANES), F32),
            pltpu.VMEM((t, LANES), F32),
            pltpu.VMEM((t, LANES), BF16),
            pltpu.VMEM((NSEG, LANES), F32),
            pltpu.VMEM((NSEG, LANES), F32),
            pltpu.VMEM((NSEG, LANES), F32),
        ],
        compiler_params=pltpu.CompilerParams(
            dimension_semantics=("parallel", "arbitrary"), vmem_limit_bytes=vmem),
        name="mixproj",
    )(hn, w_in, w_in, w_in, w_in, xa, glu, conv_a_w, row3(conv_a_b), w_rg, w_ig, row3(b_rg),
      row3(b_ig), row3(lru_lambda), conv_b_w, row3(conv_b_b))


def _from_slabs(ref):
    return jnp.concatenate([ref[s] for s in range(ref.shape[0])], axis=1)


def _out_kernel(ya_ref, ybc_ref, ssb_ref, gma_ref, gmb_ref, h_ref, wpa_ref, wpb_ref, wout_ref,
                lng_ref, lnb_ref, gn_ref, *out_refs, last):
    yc = _from_slabs(ybc_ref).astype(F32)
    mu = jnp.mean(yc, axis=-1, keepdims=True)
    xc = yc - mu
    var = jnp.mean(xc * xc, axis=-1, keepdims=True)
    ln = xc * lax.rsqrt(var + EPS) * lng_ref[...] + lnb_ref[...]
    yb = (_silu(ln) * _from_slabs(ssb_ref).astype(F32)).astype(BF16)
    pa = jnp.dot(_from_slabs(ya_ref), wpa_ref[...], preferred_element_type=F32)
    pb = jnp.dot(yb, wpb_ref[...], preferred_element_type=F32)
    merged = (_from_slabs(gma_ref).astype(F32) * pa
              + _from_slabs(gmb_ref).astype(F32) * pb).astype(BF16)
    h_new = h_ref[...] + jnp.dot(merged, wout_ref[...], preferred_element_type=F32)
    ms = jnp.mean(h_new * h_new, axis=-1, keepdims=True)
    hn = h_new * lax.rsqrt(ms + EPS) * gn_ref[...]
    if last:
        out_refs[0][...] = hn
    else:
        out_refs[0][...] = h_new
        out_refs[1][...] = hn.astype(BF16)


def _out_block(ya, ybc, ssb, gma, gmb, h, wpa, wpb, wout, layer, ln_g, ln_b, g_next, last):
    n_slabs, t, _ = ya.shape
    d = h.shape[1]
    tm = ROWS_OUT
    assert t % tm == 0
    slab = pl.BlockSpec((n_slabs, tm, LANES), lambda i: (0, i, 0))
    rows = pl.BlockSpec((tm, d), lambda i: (i, 0))
    weight = pl.BlockSpec((None, d, d), lambda i: (layer, 0, 0), pipeline_mode=pl.Buffered(1))
    vec = pl.BlockSpec((1, d), lambda i: (0, 0))
    if last:
        out_shape = [jax.ShapeDtypeStruct((t, d), F32)]
        out_specs = [rows]
    else:
        out_shape = [jax.ShapeDtypeStruct((t, d), F32), jax.ShapeDtypeStruct((t, d), BF16)]
        out_specs = [rows, rows]
    vmem = 3 * d * d * 2 + 2 * tm * d * (5 * 2 + 4 + 4 + 2) + 8 * tm * d * 4 + 4 * MIB
    return pl.pallas_call(
        functools.partial(_out_kernel, last=last),
        grid=(t // tm,),
        in_specs=[slab, slab, slab, slab, slab, rows, weight, weight, weight, vec, vec, vec],
        out_specs=out_specs,
        out_shape=out_shape,
        compiler_params=pltpu.CompilerParams(
            dimension_semantics=("parallel",), vmem_limit_bytes=vmem),
        name="out_block",
    )(ya, ybc, ssb, gma, gmb, h, wpa, wpb, wout, ln_g.reshape(1, d), ln_b.reshape(1, d),
      g_next.reshape(1, d))


def kernel(x, meta, norm_g, w_in, conv_a_w, conv_a_b, w_rg, b_rg, w_ig, b_ig, lru_lambda,
           conv_b_w, conv_b_b, ln_b_g, ln_b_b, w_proj_a, w_proj_b, w_out, final_g):
    bsz, seq, d = x.shape
    depth = w_in.shape[0]
    n_meta = meta.shape[0]
    t = n_meta + seq
    n_groups = t // NSEG
    assert t % NSEG == 0
    w_a = conv_a_w.shape[2]
    w_b = conv_b_w.shape[2]
    assert w_a == d and w_b == d
    c_sa, c_vb, c_gb, c_sb = w_a, 2 * w_a, 2 * w_a + w_b, 2 * w_a + 2 * w_b
    c_ma, c_mb = 2 * w_a + 3 * w_b, 2 * w_a + 3 * w_b + d

    wpa = w_proj_a.astype(BF16)
    wpb = w_proj_b.astype(BF16)
    wout = w_out.astype(BF16)

    outs = []
    for b in range(bsz):
        h = jnp.concatenate([meta.astype(x.dtype), x[b]], axis=0)
        h = h.reshape(NSEG, n_groups, d).transpose(1, 0, 2).reshape(t, d)
        hn = _rmsnorm(h, norm_g[0])
        for l in range(depth):
            xa = _inproj(hn, w_in, l, (0,), w_a, _identity, "inproj_xa")
            glu = _inproj(hn, w_in, l, (c_vb, c_gb), w_b, _glu, "inproj_glu")
            ssb, gma, gmb, ybc, ya = _mixproj(
                hn, w_in, xa, glu, l, (c_sa, c_sb, c_ma, c_mb), conv_a_w, conv_a_b,
                w_rg, b_rg, w_ig, b_ig, lru_lambda, conv_b_w, conv_b_b)
            last = l == depth - 1
            g_next = final_g if last else norm_g[l + 1]
            res = _out_block(ya, ybc, ssb, gma, gmb, h, wpa, wpb, wout, l, ln_b_g[l],
                             ln_b_b[l], g_next, last)
            if last:
                out = res[0]
            else:
                h, hn = res
        out = out.reshape(n_groups, NSEG, d).transpose(1, 0, 2).reshape(t, d)
        outs.append(out[n_meta:])
    return jnp.stack(outs, axis=0)
```

```python
import functools

import jax
import jax.numpy as jnp
from jax import lax
from jax.experimental import pallas as pl
from jax.experimental.pallas import tpu as pltpu

F32 = jnp.float32
BF16 = jnp.bfloat16

EPS = 1e-6
LRU_C = 8.0
LANES = 128
NSEG = 48

ROWS_INPROJ = 912
COLS_INPROJ = 1024
COLS_GLU = 512
ROWS_OUT = 144
MIX_TILE_GROUPS = 19
MIX_DOT_PARTS = 1
MIX_CONV_CHAINS = 2

MIB = 1024 * 1024


def _sigmoid(x):
    return 0.5 * jnp.tanh(0.5 * x) + 0.5


def _silu(x):
    return x * _sigmoid(x)


def _identity(x):
    return x


def _glu(v, g):
    return v * _sigmoid(g)


def _interleave_norm_kernel(x_ref, g_ref, h_ref, hn_ref):
    x = x_ref[...]
    ms = jnp.mean(x * x, axis=-1, keepdims=True)
    h_ref[...] = x
    hn_ref[...] = (x * lax.rsqrt(ms + EPS) * g_ref[...]).astype(hn_ref.dtype)


def _interleave_norm(h_tokens, g):
    t, d = h_tokens.shape
    s = t // NSEG
    out_spec = pl.BlockSpec((s, d), lambda r: (0, r))
    h, hn = pl.pallas_call(
        _interleave_norm_kernel,
        grid=(NSEG,),
        in_specs=[pl.BlockSpec((None, s, d), lambda r: (r, 0, 0)),
                  pl.BlockSpec((1, d), lambda r: (0, 0))],
        out_specs=[out_spec, out_spec],
        out_shape=[jax.ShapeDtypeStruct((s, NSEG * d), F32),
                   jax.ShapeDtypeStruct((s, NSEG * d), BF16)],
        compiler_params=pltpu.CompilerParams(dimension_semantics=("parallel",)),
        name="interleave_norm",
    )(h_tokens.reshape(NSEG, s, d), g.reshape(1, d))
    return h.reshape(t, d), hn.reshape(t, d)


def _deinterleave_kernel(x_ref, o_ref):
    o_ref[...] = x_ref[...]


def _deinterleave(y):
    t, d = y.shape
    s = t // NSEG
    out = pl.pallas_call(
        _deinterleave_kernel,
        grid=(NSEG,),
        in_specs=[pl.BlockSpec((s, d), lambda r: (0, r))],
        out_specs=pl.BlockSpec((None, s, d), lambda r: (r, 0, 0)),
        out_shape=jax.ShapeDtypeStruct((NSEG, s, d), y.dtype),
        compiler_params=pltpu.CompilerParams(dimension_semantics=("parallel",)),
        name="deinterleave",
    )(y.reshape(s, NSEG * d))
    return out.reshape(t, d)


def _inproj_kernel(hn_ref, *refs, n_w, epilogue):
    w_refs = refs[:n_w]
    o_ref = refs[n_w]
    wb_refs = refs[n_w + 1:]

    @pl.when(pl.program_id(1) == 0)
    def _cast_weights():
        for w_ref, wb_ref in zip(w_refs, wb_refs):
            wb_ref[...] = w_ref[...].astype(BF16)

    hn = hn_ref[...]
    zs = [jnp.dot(hn, wb_ref[...], preferred_element_type=F32) for wb_ref in wb_refs]
    y = epilogue(*zs).astype(o_ref.dtype)
    for s in range(o_ref.shape[0]):
        o_ref[s] = y[:, s * LANES:(s + 1) * LANES]


def _inproj(hn, w_in, layer, col_starts, n_cols, epilogue, name):
    t, d = hn.shape
    n_w = len(col_starts)
    tn = COLS_INPROJ if n_w == 1 else COLS_GLU
    tm = ROWS_INPROJ
    assert t % tm == 0 and n_cols % tn == 0 and all(c % tn == 0 for c in col_starts)
    grid = (n_cols // tn, t // tm)

    def w_spec(c0):
        return pl.BlockSpec((None, d, tn), lambda j, i: (layer, 0, c0 // tn + j))

    vmem = (2 * tm * d * 2 + n_w * (2 * d * tn * 4 + d * tn * 2)
            + 2 * tm * tn * 2 + (n_w + 2) * tm * tn * 4 + 4 * MIB)
    return pl.pallas_call(
        functools.partial(_inproj_kernel, n_w=n_w, epilogue=epilogue),
        grid=grid,
        in_specs=[pl.BlockSpec((tm, d), lambda j, i: (i, 0))] + [w_spec(c) for c in col_starts],
        out_specs=pl.BlockSpec((tn // LANES, tm, LANES), lambda j, i: (j, i, 0)),
        out_shape=jax.ShapeDtypeStruct((n_cols // LANES, t, LANES), BF16),
        scratch_shapes=[pltpu.VMEM((d, tn), BF16) for _ in range(n_w)],
        compiler_params=pltpu.CompilerParams(
            dimension_semantics=("parallel", "arbitrary"), vmem_limit_bytes=vmem),
        name=name,
    )(hn, *([w_in] * n_w))


def _left_halo(src_ref, halo, n_groups):
    tail = src_ref[pl.ds((n_groups - halo) * NSEG, halo * NSEG), :].astype(F32)
    rolled = pltpu.roll(tail, 1, 0).reshape(halo, NSEG, LANES)
    seg = lax.broadcasted_iota(jnp.int32, (halo, NSEG, LANES), 1)
    return jnp.where(seg == 0, 0.0, rolled).reshape(halo * NSEG, LANES)


def _zero_after(*arrays):
    words = []
    for a in arrays:
        bits = lax.bitcast_convert_type(a, jnp.uint32)
        words += [bits[i:i + 8] for i in range(0, a.shape[0], 8)]
    word = functools.reduce(jnp.bitwise_or, words)
    word = lax.shift_right_logical(lax.shift_right_logical(word, jnp.uint32(16)), jnp.uint32(16))
    return lax.bitcast_convert_type(word, F32)


def _order_after(lhs, anchors):
    if not anchors:
        return lhs
    zero = _zero_after(*anchors)
    zero = jnp.concatenate([zero, zero], axis=0).astype(lhs.dtype)
    head = lhs[:16] + jnp.tile(zero, (1, lhs.shape[1] // LANES))
    return jnp.concatenate([head, lhs[16:]], axis=0)


def _mixproj_kernel(hn_ref, wsa_ref, wsb_ref, wma_ref, wmb_ref, xa_ref, glu_ref,
                    caw_ref, cab_ref, wr_ref, wi_ref, br_ref, bi_ref, lam_ref, cbw_ref, cbb_ref,
                    wpa_ref, wpb_ref, wout_ref,
                    ssb_ref, gma_ref, gmb_ref, ybc_ref, ya_ref, wpa_bf_ref, wpb_bf_ref, wout_bf_ref,
                    wb_ref, xa_halo, xe_b, hloc, pcum, ssa_buf, h_state, p_state, cstart_ref):
    rt = pl.program_id(1)
    n_rt = pl.num_programs(1)
    t = xa_ref.shape[0]
    n_groups = t // NSEG
    k_a = caw_ref.shape[0]
    k_b = cbw_ref.shape[0]
    tile = MIX_TILE_GROUPS * NSEG
    r0 = pl.multiple_of(rt * tile, tile)

    @pl.when(rt == 0)
    def _head_start():
        for s, w_ref in enumerate((wsa_ref, wsb_ref, wma_ref, wmb_ref)):
            wb_ref[:, s * LANES:(s + 1) * LANES] = w_ref[...].astype(BF16)
        xe_b[pl.ds(0, (k_b - 1) * NSEG), :] = _left_halo(glu_ref, k_b - 1, n_groups)
        for w_ref, wbf_ref in ((wpa_ref, wpa_bf_ref), (wpb_ref, wpb_bf_ref), (wout_ref, wout_bf_ref)):
            wbf_ref[...] = w_ref[...].astype(BF16)

    xe_b[pl.ds((k_b - 1) * NSEG + r0, tile), :] = glu_ref[pl.ds(r0, tile), :].astype(F32)

    w_gate = jnp.concatenate([wr_ref[...], wi_ref[...]], axis=1).astype(BF16)
    neg_lam = -lam_ref[...]
    softplus = jnp.maximum(neg_lam, 0.0) + jnp.log1p(jnp.exp(-jnp.abs(neg_lam)))
    c_lru = -LRU_C * softplus
    caw = caw_ref[...]
    halo_a = (k_a - 1) * NSEG
    xa_tile = xa_ref[pl.ds(r0, tile), :].astype(F32)
    xa_prev = jnp.where(rt == 0, _left_halo(xa_ref, k_a - 1, n_groups), xa_halo[...])
    xa_halo[...] = xa_tile[tile - halo_a:]
    xa_in = jnp.concatenate([xa_prev, xa_tile], axis=0)
    y = cab_ref[...] + caw[0:1] * xa_in[:tile]
    for k in range(1, k_a):
        y = y + caw[k:k + 1] * xa_in[k * NSEG:k * NSEG + tile]
    g = jnp.dot(y.astype(BF16), w_gate, preferred_element_type=F32)

    h = jnp.where(rt == 0, 0.0, h_state[...])
    p = jnp.where(rt == 0, 1.0, p_state[...])
    b_r = jnp.broadcast_to(br_ref[...], (8, LANES))
    for jj in range(MIX_TILE_GROUPS):
        rows = slice(jj * NSEG, (jj + 1) * NSEG)
        r = _sigmoid(g[rows, :LANES] + jnp.tile(b_r, (NSEG // 8, 1)))
        i = _sigmoid(g[rows, LANES:] + bi_ref[...])
        log_a = c_lru * r
        a_j = jnp.exp(log_a)
        mult = jnp.sqrt(jnp.maximum(-jnp.tanh(log_a) * (a_j * a_j + 1.0), 0.0))
        h = a_j * h + mult * (i * y[rows])
        p = a_j * p
        hloc[pl.ds(r0 + jj * NSEG, NSEG), :] = h
        pcum[pl.ds(r0 + jj * NSEG, NSEG), :] = p
        b_r = br_ref[...] + _zero_after(h)
    h_state[...] = h
    p_state[...] = p

    cbw = cbw_ref[...]
    w_rows = [jnp.broadcast_to(cbw[k:k + 1], (8, LANES)) for k in range(k_b)]
    bias_b = jnp.broadcast_to(cbb_ref[...], (8, LANES))
    chain_tail = [None] * MIX_CONV_CHAINS
    for jj in range(MIX_TILE_GROUPS):
        accs = []
        for s in range(NSEG // 8):
            lane = (jj * (NSEG // 8) + s) % MIX_CONV_CHAINS
            acc = bias_b if chain_tail[lane] is None else bias_b + _zero_after(chain_tail[lane])
            for k in range(k_b):
                acc = acc + w_rows[k] * xe_b[pl.ds(r0 + (jj + k) * NSEG + s * 8, 8), :]
            chain_tail[lane] = acc
            accs.append(acc)
        ybc_ref[jj * NSEG:(jj + 1) * NSEG, :] = jnp.concatenate(accs, axis=0).astype(BF16)

    part = tile // MIX_DOT_PARTS
    n_pieces = 2 * MIX_DOT_PARTS
    anchors = [[] for _ in range(n_pieces)]
    piece = 0
    for m0 in range(0, tile, part):
        for half, (act, dsts) in enumerate((
                (_silu, ((ssa_buf, r0 + m0), (ssb_ref, m0))),
                (_sigmoid, ((gma_ref, m0), (gmb_ref, m0))))):
            hn = _order_after(hn_ref[m0:m0 + part, :], anchors[piece])
            z = jnp.dot(hn, wb_ref[:, half * 2 * LANES:(half + 1) * 2 * LANES],
                        preferred_element_type=F32)
            for s, (dst, row) in enumerate(dsts):
                out = act(z[:, s * LANES:(s + 1) * LANES])
                dst[pl.ds(row, part), :] = out.astype(BF16)
                if piece + 2 < n_pieces:
                    anchors[piece + 2].append(out)
            piece += 1

    @pl.when(rt == n_rt - 1)
    def _head_end():
        c = jnp.zeros((1, LANES), F32)
        for seg in range(NSEG):
            cstart_ref[pl.ds(seg, 1), :] = c
            c = h_state[pl.ds(seg, 1), :] + p_state[pl.ds(seg, 1), :] * c
        c_start = cstart_ref[...]

        def fix_step(j, carry):
            q0 = pl.multiple_of(j * NSEG, NSEG)
            hh = hloc[pl.ds(q0, NSEG), :] + pcum[pl.ds(q0, NSEG), :] * c_start
            ya_ref[pl.ds(q0, NSEG), :] = (hh * ssa_buf[pl.ds(q0, NSEG), :].astype(F32)).astype(BF16)
            return carry

        lax.fori_loop(0, n_groups, fix_step, 0)


def _mixproj(hn, w_in, xa, glu, layer, gate_cols, conv_a_w, conv_a_b, w_rg, b_rg, w_ig, b_ig,
             lru_lambda, conv_b_w, conv_b_b, proj_weights):
    n_heads, t, _ = xa.shape
    d = hn.shape[1]
    k_a = conv_a_w.shape[1]
    k_b = conv_b_w.shape[1]
    n_groups = t // NSEG
    tile = MIX_TILE_GROUPS * NSEG
    assert t % NSEG == 0 and n_groups % MIX_TILE_GROUPS == 0 and n_groups >= k_b - 1
    assert all(c % LANES == 0 for c in gate_cols)
    n_proj = len(proj_weights)
    assert all(w.shape[1:] == (n_heads * LANES, d) for w in proj_weights)

    def w_strip(c0):
        return pl.BlockSpec((None, d, LANES), lambda c, i: (layer, 0, c0 // LANES + c))

    slab = pl.BlockSpec((None, t, LANES), lambda c, i: (c, 0, 0))
    slab_tile = pl.BlockSpec((None, tile, LANES), lambda c, i: (c, i, 0))

    def per_channel(k):
        return pl.BlockSpec((None, k, LANES), lambda c, i: (layer, 0, c))

    gate_w = pl.BlockSpec((None, None, LANES, LANES), lambda c, i: (layer, c, 0, 0))
    slab_shape = jax.ShapeDtypeStruct(xa.shape, BF16)
    row3 = lambda p: p.reshape(p.shape[0], 1, p.shape[1])
    vmem = (2 * tile * d * 2 + 4 * 2 * d * LANES * 4 + d * 4 * LANES * 2
            + 2 * 2 * t * LANES * 2 + (3 * t + (k_a + k_b - 2) * NSEG) * LANES * 4
            + t * LANES * 2 + 2 * t * LANES * 2 + 4 * 2 * tile * LANES * 2
            + n_proj * 2 * LANES * d * (4 + 2)
            + 10 * tile * LANES * 4 + 4 * MIB)
    return pl.pallas_call(
        _mixproj_kernel,
        grid=(n_heads, n_groups // MIX_TILE_GROUPS),
        in_specs=[pl.BlockSpec((tile, d), lambda c, i: (i, 0))]
                 + [w_strip(c0) for c0 in gate_cols]
                 + [slab, slab, per_channel(k_a), per_channel(1), gate_w, gate_w,
                    per_channel(1), per_channel(1), per_channel(1), per_channel(k_b),
                    per_channel(1)]
                 + [pl.BlockSpec((None, LANES, d), lambda c, i: (layer, c, 0))] * n_proj,
        out_specs=[slab_tile, slab_tile, slab_tile, slab_tile, slab]
                  + [pl.BlockSpec((LANES, d), lambda c, i: (c, 0))] * n_proj,
        out_shape=[slab_shape] * 5 + [jax.ShapeDtypeStruct((d, d), BF16)] * n_proj,
        scratch_shapes=[
            pltpu.VMEM((d, 4 * LANES), BF16),
            pltpu.VMEM(((k_a - 1) * NSEG, LANES), F32),
            pltpu.VMEM(((k_b - 1 + n_groups) * NSEG, LANES), F32),
            pltpu.VMEM((t, LANES), F32),
            pltpu.VMEM((t, LANES), F32),
            pltpu.VMEM((t, LANES), BF16),
            pltpu.VMEM((NSEG, LANES), F32),
            pltpu.VMEM((NSEG, LANES), F32),
            pltpu.VMEM((NSEG, LANES), F32),
        ],
        compiler_params=pltpu.CompilerParams(
            dimension_semantics=("parallel", "arbitrary"), vmem_limit_bytes=vmem),
        name="mixproj",
    )(hn, w_in, w_in, w_in, w_in, xa, glu, conv_a_w, row3(conv_a_b), w_rg, w_ig, row3(b_rg),
      row3(b_ig), row3(lru_lambda), conv_b_w, row3(conv_b_b), *proj_weights)


def _from_slabs(ref):
    return jnp.concatenate([ref[s] for s in range(ref.shape[0])], axis=1)


def _out_kernel(ya_ref, ybc_ref, ssb_ref, gma_ref, gmb_ref, h_ref, wpa_ref, wpb_ref, wout_ref,
                lng_ref, lnb_ref, gn_ref, *out_refs, last):
    yc = _from_slabs(ybc_ref).astype(F32)
    mu = jnp.mean(yc, axis=-1, keepdims=True)
    xc = yc - mu
    var = jnp.mean(xc * xc, axis=-1, keepdims=True)
    ln = xc * lax.rsqrt(var + EPS) * lng_ref[...] + lnb_ref[...]
    yb = (_silu(ln) * _from_slabs(ssb_ref).astype(F32)).astype(BF16)
    pa = jnp.dot(_from_slabs(ya_ref), wpa_ref[...], preferred_element_type=F32)
    pb = jnp.dot(yb, wpb_ref[...], preferred_element_type=F32)
    merged = (_from_slabs(gma_ref).astype(F32) * pa
              + _from_slabs(gmb_ref).astype(F32) * pb).astype(BF16)
    h_new = h_ref[...] + jnp.dot(merged, wout_ref[...], preferred_element_type=F32)
    ms = jnp.mean(h_new * h_new, axis=-1, keepdims=True)
    hn = h_new * lax.rsqrt(ms + EPS) * gn_ref[...]
    if last:
        out_refs[0][...] = hn
    else:
        out_refs[0][...] = h_new
        out_refs[1][...] = hn.astype(BF16)


def _out_block(ya, ybc, ssb, gma, gmb, h, wpa, wpb, wout, ln_g, ln_b, g_next, last):
    n_slabs, t, _ = ya.shape
    d = h.shape[1]
    tm = ROWS_OUT
    assert t % tm == 0
    slab = pl.BlockSpec((n_slabs, tm, LANES), lambda i: (0, i, 0))
    rows = pl.BlockSpec((tm, d), lambda i: (i, 0))
    weight = pl.BlockSpec((d, d), lambda i: (0, 0), pipeline_mode=pl.Buffered(1))
    vec = pl.BlockSpec((1, d), lambda i: (0, 0))
    if last:
        out_shape = [jax.ShapeDtypeStruct((t, d), F32)]
        out_specs = [rows]
    else:
        out_shape = [jax.ShapeDtypeStruct((t, d), F32), jax.ShapeDtypeStruct((t, d), BF16)]
        out_specs = [rows, rows]
    vmem = 3 * d * d * 2 + 2 * tm * d * (5 * 2 + 4 + 4 + 2) + 8 * tm * d * 4 + 4 * MIB
    return pl.pallas_call(
        functools.partial(_out_kernel, last=last),
        grid=(t // tm,),
        in_specs=[slab, slab, slab, slab, slab, rows, weight, weight, weight, vec, vec, vec],
        out_specs=out_specs,
        out_shape=out_shape,
        compiler_params=pltpu.CompilerParams(
            dimension_semantics=("parallel",), vmem_limit_bytes=vmem),
        name="out_block",
    )(ya, ybc, ssb, gma, gmb, h, wpa, wpb, wout, ln_g.reshape(1, d), ln_b.reshape(1, d),
      g_next.reshape(1, d))


def kernel(x, meta, norm_g, w_in, conv_a_w, conv_a_b, w_rg, b_rg, w_ig, b_ig, lru_lambda,
           conv_b_w, conv_b_b, ln_b_g, ln_b_b, w_proj_a, w_proj_b, w_out, final_g):
    bsz, seq, d = x.shape
    depth = w_in.shape[0]
    n_meta = meta.shape[0]
    t = n_meta + seq
    n_groups = t // NSEG
    assert t % NSEG == 0
    w_a = conv_a_w.shape[2]
    w_b = conv_b_w.shape[2]
    assert w_a == d and w_b == d
    c_sa, c_vb, c_gb, c_sb = w_a, 2 * w_a, 2 * w_a + w_b, 2 * w_a + 2 * w_b
    c_ma, c_mb = 2 * w_a + 3 * w_b, 2 * w_a + 3 * w_b + d

    outs = []
    for b in range(bsz):
        h_tokens = jnp.concatenate([meta.astype(x.dtype), x[b]], axis=0)
        h, hn = _interleave_norm(h_tokens, norm_g[0])
        for l in range(depth):
            xa = _inproj(hn, w_in, l, (0,), w_a, _identity, "inproj_xa")
            glu = _inproj(hn, w_in, l, (c_vb, c_gb), w_b, _glu, "inproj_glu")
            ssb, gma, gmb, ybc, ya, wpa, wpb, wout = _mixproj(
                hn, w_in, xa, glu, l, (c_sa, c_sb, c_ma, c_mb), conv_a_w, conv_a_b,
                w_rg, b_rg, w_ig, b_ig, lru_lambda, conv_b_w, conv_b_b,
                (w_proj_a, w_proj_b, w_out))
            last = l == depth - 1
            g_next = final_g if last else norm_g[l + 1]
            res = _out_block(ya, ybc, ssb, gma, gmb, h, wpa, wpb, wout, ln_b_g[l],
                             ln_b_b[l], g_next, last)
            if last:
                out = res[0]
            else:
                h, hn = res
        outs.append(_deinterleave(out)[n_meta:])
    return jnp.stack(outs, axis=0)
```

```python
import functools

import jax
import jax.numpy as jnp
from jax import lax
from jax.experimental import pallas as pl
from jax.experimental.pallas import tpu as pltpu

F32 = jnp.float32
BF16 = jnp.bfloat16

EPS = 1e-6
LRU_C = 8.0
LANES = 128
NSEG = 48

ROWS_INPROJ = 912
COLS_INPROJ = 1024
COLS_GLU = 512
ROWS_OUT = 144
ROWS_NORM = 432
MIX_TILE_GROUPS = 19
MIX_DOT_PARTS = 1
MIX_GATE_SPLIT = 3
MIX_CONV_CHAINS = 2

MIB = 1024 * 1024


def _sigmoid(x):
    return 0.5 * jnp.tanh(0.5 * x) + 0.5


def _silu(x):
    return x * _sigmoid(x)


def _identity(x):
    return x


def _glu(v, g):
    return v * _sigmoid(g)


def _rmsnorm_kernel(x_ref, g_ref, o_ref):
    x = x_ref[...]
    ms = jnp.mean(x * x, axis=-1, keepdims=True)
    o_ref[...] = (x * lax.rsqrt(ms + EPS) * g_ref[...]).astype(o_ref.dtype)


def _rmsnorm(h, g):
    t, d = h.shape
    return pl.pallas_call(
        _rmsnorm_kernel,
        grid=(t // ROWS_NORM,),
        in_specs=[pl.BlockSpec((ROWS_NORM, d), lambda i: (i, 0)),
                  pl.BlockSpec((1, d), lambda i: (0, 0))],
        out_specs=pl.BlockSpec((ROWS_NORM, d), lambda i: (i, 0)),
        out_shape=jax.ShapeDtypeStruct((t, d), BF16),
        compiler_params=pltpu.CompilerParams(dimension_semantics=("parallel",)),
        name="rmsnorm0",
    )(h, g.reshape(1, d))


def _inproj_kernel(hn_ref, *refs, n_w, epilogue):
    w_refs = refs[:n_w]
    o_ref = refs[n_w]
    wb_refs = refs[n_w + 1:]

    @pl.when(pl.program_id(1) == 0)
    def _cast_weights():
        for w_ref, wb_ref in zip(w_refs, wb_refs):
            wb_ref[...] = w_ref[...].astype(BF16)

    hn = hn_ref[...]
    zs = [jnp.dot(hn, wb_ref[...], preferred_element_type=F32) for wb_ref in wb_refs]
    y = epilogue(*zs).astype(o_ref.dtype)
    for s in range(o_ref.shape[0]):
        o_ref[s] = y[:, s * LANES:(s + 1) * LANES]


def _inproj(hn, w_in, layer, col_starts, n_cols, epilogue, name):
    t, d = hn.shape
    n_w = len(col_starts)
    tn = COLS_INPROJ if n_w == 1 else COLS_GLU
    tm = ROWS_INPROJ
    assert t % tm == 0 and n_cols % tn == 0 and all(c % tn == 0 for c in col_starts)
    grid = (n_cols // tn, t // tm)

    def w_spec(c0):
        return pl.BlockSpec((None, d, tn), lambda j, i: (layer, 0, c0 // tn + j))

    vmem = (2 * tm * d * 2 + n_w * (2 * d * tn * 4 + d * tn * 2)
            + 2 * tm * tn * 2 + (n_w + 2) * tm * tn * 4 + 4 * MIB)
    return pl.pallas_call(
        functools.partial(_inproj_kernel, n_w=n_w, epilogue=epilogue),
        grid=grid,
        in_specs=[pl.BlockSpec((tm, d), lambda j, i: (i, 0))] + [w_spec(c) for c in col_starts],
        out_specs=pl.BlockSpec((tn // LANES, tm, LANES), lambda j, i: (j, i, 0)),
        out_shape=jax.ShapeDtypeStruct((n_cols // LANES, t, LANES), BF16),
        scratch_shapes=[pltpu.VMEM((d, tn), BF16) for _ in range(n_w)],
        compiler_params=pltpu.CompilerParams(
            dimension_semantics=("parallel", "arbitrary"), vmem_limit_bytes=vmem),
        name=name,
    )(hn, *([w_in] * n_w))


def _left_halo(src_ref, halo, n_groups):
    tail = src_ref[pl.ds((n_groups - halo) * NSEG, halo * NSEG), :].astype(F32)
    rolled = pltpu.roll(tail, 1, 0).reshape(halo, NSEG, LANES)
    seg = lax.broadcasted_iota(jnp.int32, (halo, NSEG, LANES), 1)
    return jnp.where(seg == 0, 0.0, rolled).reshape(halo * NSEG, LANES)


def _zero_after(*arrays):
    words = []
    for a in arrays:
        bits = lax.bitcast_convert_type(a, jnp.uint32)
        words += [bits[i:i + 8] for i in range(0, a.shape[0], 8)]
    word = functools.reduce(jnp.bitwise_or, words)
    word = lax.shift_right_logical(lax.shift_right_logical(word, jnp.uint32(16)), jnp.uint32(16))
    return lax.bitcast_convert_type(word, F32)


def _order_after(lhs, anchors):
    if not anchors:
        return lhs
    zero = _zero_after(*anchors)
    zero = jnp.concatenate([zero, zero], axis=0).astype(lhs.dtype)
    head = lhs[:16] + jnp.tile(zero, (1, lhs.shape[1] // LANES))
    return jnp.concatenate([head, lhs[16:]], axis=0)


def _mixproj_kernel(hn_ref, wsa_ref, wsb_ref, wma_ref, wmb_ref, xa_ref, glu_ref,
                    caw_ref, cab_ref, wr_ref, wi_ref, br_ref, bi_ref, lam_ref, cbw_ref, cbb_ref,
                    wpa_ref, wpb_ref, wout_ref,
                    ssb_ref, gma_ref, gmb_ref, ybc_ref, ya_ref, wpa_bf_ref, wpb_bf_ref, wout_bf_ref,
                    wb_ref, xa_halo, xe_b, hloc, pcum, ssa_buf, h_state, p_state, cstart_ref):
    rt = pl.program_id(1)
    n_rt = pl.num_programs(1)
    t = xa_ref.shape[0]
    n_groups = t // NSEG
    k_a = caw_ref.shape[0]
    k_b = cbw_ref.shape[0]
    tile = MIX_TILE_GROUPS * NSEG
    r0 = pl.multiple_of(rt * tile, tile)

    @pl.when(rt == 0)
    def _head_start():
        for s, w_ref in enumerate((wsa_ref, wsb_ref, wma_ref, wmb_ref)):
            wb_ref[:, s * LANES:(s + 1) * LANES] = w_ref[...].astype(BF16)
        xe_b[pl.ds(0, (k_b - 1) * NSEG), :] = _left_halo(glu_ref, k_b - 1, n_groups)
        for w_ref, wbf_ref in ((wpa_ref, wpa_bf_ref), (wpb_ref, wpb_bf_ref), (wout_ref, wout_bf_ref)):
            wbf_ref[...] = w_ref[...].astype(BF16)

    xe_b[pl.ds((k_b - 1) * NSEG + r0, tile), :] = glu_ref[pl.ds(r0, tile), :].astype(F32)

    w_gate = jnp.concatenate([wr_ref[...], wi_ref[...]], axis=1).astype(BF16)
    neg_lam = -lam_ref[...]
    softplus = jnp.maximum(neg_lam, 0.0) + jnp.log1p(jnp.exp(-jnp.abs(neg_lam)))
    c_lru = -LRU_C * softplus
    caw = caw_ref[...]
    halo_a = (k_a - 1) * NSEG
    xa_tile = xa_ref[pl.ds(r0, tile), :].astype(F32)
    xa_prev = jnp.where(rt == 0, _left_halo(xa_ref, k_a - 1, n_groups), xa_halo[...])
    xa_halo[...] = xa_tile[tile - halo_a:]
    xa_in = jnp.concatenate([xa_prev, xa_tile], axis=0)
    y = cab_ref[...] + caw[0:1] * xa_in[:tile]
    for k in range(1, k_a):
        y = y + caw[k:k + 1] * xa_in[k * NSEG:k * NSEG + tile]
    g = jnp.dot(y.astype(BF16), w_gate, preferred_element_type=F32)

    h_all = jnp.where(rt == 0, 0.0, h_state[...])
    p_all = jnp.where(rt == 0, 1.0, p_state[...])
    sub = NSEG // MIX_GATE_SPLIT
    h = [h_all[q * sub:(q + 1) * sub] for q in range(MIX_GATE_SPLIT)]
    p = [p_all[q * sub:(q + 1) * sub] for q in range(MIX_GATE_SPLIT)]
    b_r = jnp.broadcast_to(br_ref[...], (8, LANES))
    for jj in range(MIX_TILE_GROUPS):
        for q in range(MIX_GATE_SPLIT):
            rows = slice(jj * NSEG + q * sub, jj * NSEG + (q + 1) * sub)
            r = _sigmoid(g[rows, :LANES] + jnp.tile(b_r, (sub // 8, 1)))
            i = _sigmoid(g[rows, LANES:] + bi_ref[...])
            log_a = c_lru * r
            a_j = jnp.exp(log_a)
            mult = jnp.sqrt(jnp.maximum(-jnp.tanh(log_a) * (a_j * a_j + 1.0), 0.0))
            h[q] = a_j * h[q] + mult * (i * y[rows])
            p[q] = a_j * p[q]
            hloc[pl.ds(r0 + jj * NSEG + q * sub, sub), :] = h[q]
            pcum[pl.ds(r0 + jj * NSEG + q * sub, sub), :] = p[q]
            b_r = br_ref[...] + _zero_after(h[q])
    h_state[...] = jnp.concatenate(h, axis=0)
    p_state[...] = jnp.concatenate(p, axis=0)

    cbw = cbw_ref[...]
    w_rows = [jnp.broadcast_to(cbw[k:k + 1], (8, LANES)) for k in range(k_b)]
    bias_b = jnp.broadcast_to(cbb_ref[...], (8, LANES))
    chain_tail = [None] * MIX_CONV_CHAINS
    for jj in range(MIX_TILE_GROUPS):
        accs = []
        for s in range(NSEG // 8):
            lane = (jj * (NSEG // 8) + s) % MIX_CONV_CHAINS
            acc = bias_b if chain_tail[lane] is None else bias_b + _zero_after(chain_tail[lane])
            for k in range(k_b):
                acc = acc + w_rows[k] * xe_b[pl.ds(r0 + (jj + k) * NSEG + s * 8, 8), :]
            chain_tail[lane] = acc
            accs.append(acc)
        ybc_ref[jj * NSEG:(jj + 1) * NSEG, :] = jnp.concatenate(accs, axis=0).astype(BF16)

    part = tile // MIX_DOT_PARTS
    n_pieces = 2 * MIX_DOT_PARTS
    anchors = [[] for _ in range(n_pieces)]
    piece = 0
    for m0 in range(0, tile, part):
        for half, (acts, dsts) in enumerate((
                ((_silu, _silu), ((ssa_buf, r0 + m0), (ssb_ref, m0))),
                ((_sigmoid, _sigmoid), ((gma_ref, m0), (gmb_ref, m0))))):
            hn = _order_after(hn_ref[m0:m0 + part, :], anchors[piece])
            z = jnp.dot(hn, wb_ref[:, half * 2 * LANES:(half + 1) * 2 * LANES],
                        preferred_element_type=F32)
            for s, (dst, row) in enumerate(dsts):
                out = acts[s](z[:, s * LANES:(s + 1) * LANES])
                dst[pl.ds(row, part), :] = out.astype(BF16)
                if piece + 2 < n_pieces:
                    anchors[piece + 2].append(out)
            piece += 1

    @pl.when(rt == n_rt - 1)
    def _head_end():
        c = jnp.zeros((1, LANES), F32)
        for seg in range(NSEG):
            cstart_ref[pl.ds(seg, 1), :] = c
            c = h_state[pl.ds(seg, 1), :] + p_state[pl.ds(seg, 1), :] * c
        c_start = cstart_ref[...]

        def fix_step(j, carry):
            q0 = pl.multiple_of(j * NSEG, NSEG)
            hh = hloc[pl.ds(q0, NSEG), :] + pcum[pl.ds(q0, NSEG), :] * c_start
            ya_ref[pl.ds(q0, NSEG), :] = (hh * ssa_buf[pl.ds(q0, NSEG), :].astype(F32)).astype(BF16)
            return carry

        lax.fori_loop(0, n_groups, fix_step, 0, unroll=3 if n_groups % 3 == 0 else 1)


def _mixproj(hn, w_in, xa, glu, layer, gate_cols, conv_a_w, conv_a_b, w_rg, b_rg, w_ig, b_ig,
             lru_lambda, conv_b_w, conv_b_b, proj_weights):
    n_heads, t, _ = xa.shape
    d = hn.shape[1]
    k_a = conv_a_w.shape[1]
    k_b = conv_b_w.shape[1]
    n_groups = t // NSEG
    tile = MIX_TILE_GROUPS * NSEG
    assert t % NSEG == 0 and n_groups % MIX_TILE_GROUPS == 0 and n_groups >= k_b - 1
    assert all(c % LANES == 0 for c in gate_cols)
    n_proj = len(proj_weights)
    assert all(w.shape[1:] == (n_heads * LANES, d) for w in proj_weights)

    def w_strip(c0):
        return pl.BlockSpec((None, d, LANES), lambda c, i: (layer, 0, c0 // LANES + c))

    slab = pl.BlockSpec((None, t, LANES), lambda c, i: (c, 0, 0))
    slab_tile = pl.BlockSpec((None, tile, LANES), lambda c, i: (c, i, 0))

    def per_channel(k):
        return pl.BlockSpec((None, k, LANES), lambda c, i: (layer, 0, c))

    gate_w = pl.BlockSpec((None, None, LANES, LANES), lambda c, i: (layer, c, 0, 0))
    slab_shape = jax.ShapeDtypeStruct(xa.shape, BF16)
    row3 = lambda p: p.reshape(p.shape[0], 1, p.shape[1])
    vmem = (2 * tile * d * 2 + 4 * 2 * d * LANES * 4 + d * 4 * LANES * 2
            + 2 * 2 * t * LANES * 2 + (3 * t + (k_a + k_b - 2) * NSEG) * LANES * 4
            + t * LANES * 2 + 2 * t * LANES * 2 + 4 * 2 * tile * LANES * 2
            + n_proj * 2 * LANES * d * (4 + 2)
            + 10 * tile * LANES * 4 + 4 * MIB)
    return pl.pallas_call(
        _mixproj_kernel,
        grid=(n_heads, n_groups // MIX_TILE_GROUPS),
        in_specs=[pl.BlockSpec((tile, d), lambda c, i: (i, 0))]
                 + [w_strip(c0) for c0 in gate_cols]
                 + [slab, slab, per_channel(k_a), per_channel(1), gate_w, gate_w,
                    per_channel(1), per_channel(1), per_channel(1), per_channel(k_b),
                    per_channel(1)]
                 + [pl.BlockSpec((None, LANES, d), lambda c, i: (layer, c, 0))] * n_proj,
        out_specs=[slab_tile, slab_tile, slab_tile, slab_tile, slab]
                  + [pl.BlockSpec((LANES, d), lambda c, i: (c, 0))] * n_proj,
        out_shape=[slab_shape] * 5 + [jax.ShapeDtypeStruct((d, d), BF16)] * n_proj,
        scratch_shapes=[
            pltpu.VMEM((d, 4 * LANES), BF16),
            pltpu.VMEM(((k_a - 1) * NSEG, LANES), F32),
            pltpu.VMEM(((k_b - 1 + n_groups) * NSEG, LANES), F32),
            pltpu.VMEM((t, LANES), F32),
            pltpu.VMEM((t, LANES), F32),
            pltpu.VMEM((t, LANES), BF16),
            pltpu.VMEM((NSEG, LANES), F32),
            pltpu.VMEM((NSEG, LANES), F32),
            pltpu.VMEM((NSEG, LANES), F32),
        ],
        compiler_params=pltpu.CompilerParams(
            dimension_semantics=("parallel", "arbitrary"), vmem_limit_bytes=vmem),
        name="mixproj",
    )(hn, w_in, w_in, w_in, w_in, xa, glu, conv_a_w, row3(conv_a_b), w_rg, w_ig, row3(b_rg),
      row3(b_ig), row3(lru_lambda), conv_b_w, row3(conv_b_b), *proj_weights)


def _from_slabs(ref):
    return jnp.concatenate([ref[s] for s in range(ref.shape[0])], axis=1)


def _out_kernel(ya_ref, ybc_ref, ssb_ref, gma_ref, gmb_ref, h_ref, wpa_ref, wpb_ref, wout_ref,
                lng_ref, lnb_ref, gn_ref, *out_refs, last):
    yc = _from_slabs(ybc_ref).astype(F32)
    mu = jnp.mean(yc, axis=-1, keepdims=True)
    xc = yc - mu
    var = jnp.mean(xc * xc, axis=-1, keepdims=True)
    ln = xc * lax.rsqrt(var + EPS) * lng_ref[...] + lnb_ref[...]
    yb = (_silu(ln) * _from_slabs(ssb_ref).astype(F32)).astype(BF16)
    pa = jnp.dot(_from_slabs(ya_ref), wpa_ref[...], preferred_element_type=F32)
    pb = jnp.dot(yb, wpb_ref[...], preferred_element_type=F32)
    merged = (_from_slabs(gma_ref).astype(F32) * pa
              + _from_slabs(gmb_ref).astype(F32) * pb).astype(BF16)
    h_new = h_ref[...] + jnp.dot(merged, wout_ref[...], preferred_element_type=F32)
    ms = jnp.mean(h_new * h_new, axis=-1, keepdims=True)
    hn = h_new * lax.rsqrt(ms + EPS) * gn_ref[...]
    if last:
        out_refs[0][...] = hn
    else:
        out_refs[0][...] = h_new
        out_refs[1][...] = hn.astype(BF16)


def _out_block(ya, ybc, ssb, gma, gmb, h, wpa, wpb, wout, ln_g, ln_b, g_next, last):
    n_slabs, t, _ = ya.shape
    d = h.shape[1]
    tm = ROWS_OUT
    assert t % tm == 0
    slab = pl.BlockSpec((n_slabs, tm, LANES), lambda i: (0, i, 0))
    rows = pl.BlockSpec((tm, d), lambda i: (i, 0))
    weight = pl.BlockSpec((d, d), lambda i: (0, 0), pipeline_mode=pl.Buffered(1))
    vec = pl.BlockSpec((1, d), lambda i: (0, 0))
    if last:
        out_shape = [jax.ShapeDtypeStruct((t, d), F32)]
        out_specs = [rows]
    else:
        out_shape = [jax.ShapeDtypeStruct((t, d), F32), jax.ShapeDtypeStruct((t, d), BF16)]
        out_specs = [rows, rows]
    vmem = 3 * d * d * 2 + 2 * tm * d * (5 * 2 + 4 + 4 + 2) + 8 * tm * d * 4 + 4 * MIB
    return pl.pallas_call(
        functools.partial(_out_kernel, last=last),
        grid=(t // tm,),
        in_specs=[slab, slab, slab, slab, slab, rows, weight, weight, weight, vec, vec, vec],
        out_specs=out_specs,
        out_shape=out_shape,
        compiler_params=pltpu.CompilerParams(
            dimension_semantics=("parallel",), vmem_limit_bytes=vmem),
        name="out_block",
    )(ya, ybc, ssb, gma, gmb, h, wpa, wpb, wout, ln_g.reshape(1, d), ln_b.reshape(1, d),
      g_next.reshape(1, d))


def kernel(x, meta, norm_g, w_in, conv_a_w, conv_a_b, w_rg, b_rg, w_ig, b_ig, lru_lambda,
           conv_b_w, conv_b_b, ln_b_g, ln_b_b, w_proj_a, w_proj_b, w_out, final_g):
    bsz, seq, d = x.shape
    depth = w_in.shape[0]
    n_meta = meta.shape[0]
    t = n_meta + seq
    n_groups = t // NSEG
    assert t % NSEG == 0
    w_a = conv_a_w.shape[2]
    w_b = conv_b_w.shape[2]
    assert w_a == d and w_b == d
    c_sa, c_vb, c_gb, c_sb = w_a, 2 * w_a, 2 * w_a + w_b, 2 * w_a + 2 * w_b
    c_ma, c_mb = 2 * w_a + 3 * w_b, 2 * w_a + 3 * w_b + d

    outs = []
    for b in range(bsz):
        h = jnp.concatenate([meta.astype(x.dtype), x[b]], axis=0)
        h = h.reshape(NSEG, n_groups, d).transpose(1, 0, 2).reshape(t, d)
        hn = _rmsnorm(h, norm_g[0])
        for l in range(depth):
            xa = _inproj(hn, w_in, l, (0,), w_a, _identity, "inproj_xa")
            glu = _inproj(hn, w_in, l, (c_vb, c_gb), w_b, _glu, "inproj_glu")
            ssb, gma, gmb, ybc, ya, wpa, wpb, wout = _mixproj(
                hn, w_in, xa, glu, l, (c_sa, c_sb, c_ma, c_mb), conv_a_w, conv_a_b,
                w_rg, b_rg, w_ig, b_ig, lru_lambda, conv_b_w, conv_b_b,
                (w_proj_a, w_proj_b, w_out))
            last = l == depth - 1
            g_next = final_g if last else norm_g[l + 1]
            res = _out_block(ya, ybc, ssb, gma, gmb, h, wpa, wpb, wout, ln_b_g[l],
                             ln_b_b[l], g_next, last)
            if last:
                out = res[0]
            else:
                h, hn = res
        out = out.reshape(n_groups, NSEG, d).transpose(1, 0, 2).reshape(t, d)
        outs.append(out[n_meta:])
    return outs[0][None] if bsz == 1 else jnp.stack(outs, axis=0)
```

```python
import functools

import jax
import jax.numpy as jnp
from jax import lax
from jax.experimental import pallas as pl
from jax.experimental.pallas import tpu as pltpu

F32 = jnp.float32
BF16 = jnp.bfloat16

EPS = 1e-6
LRU_C = 8.0
LANES = 128
NSEG = 48

ROWS_INPROJ = 912
COLS_INPROJ = 1024
COLS_GLU = 512
ROWS_OUT = 144
ROWS_NORM = 432
MIX_TILE_GROUPS = 19
MIX_DOT_PARTS = 1
MIX_GATE_SPLIT = 3
MIX_CONV_CHAINS = 2

MIB = 1024 * 1024


def _sigmoid(x):
    return 0.5 * jnp.tanh(0.5 * x) + 0.5


def _silu(x):
    return x * _sigmoid(x)


def _identity(x):
    return x


def _glu(v, g):
    return v * _sigmoid(g)


def _rmsnorm_kernel(x_ref, g_ref, o_ref):
    x = x_ref[...]
    ms = jnp.mean(x * x, axis=-1, keepdims=True)
    o_ref[...] = (x * lax.rsqrt(ms + EPS) * g_ref[...]).astype(o_ref.dtype)


def _rmsnorm(h, g):
    t, d = h.shape
    return pl.pallas_call(
        _rmsnorm_kernel,
        grid=(t // ROWS_NORM,),
        in_specs=[pl.BlockSpec((ROWS_NORM, d), lambda i: (i, 0)),
                  pl.BlockSpec((1, d), lambda i: (0, 0))],
        out_specs=pl.BlockSpec((ROWS_NORM, d), lambda i: (i, 0)),
        out_shape=jax.ShapeDtypeStruct((t, d), BF16),
        compiler_params=pltpu.CompilerParams(dimension_semantics=("parallel",)),
        name="rmsnorm0",
    )(h, g.reshape(1, d))


def _inproj_kernel(hn_ref, *refs, n_w, n_cast, epilogue):
    w_refs = refs[:n_w]
    cast_in = refs[n_w:n_w + n_cast]
    o_ref = refs[n_w + n_cast]
    cast_out = refs[n_w + n_cast + 1:n_w + 2 * n_cast + 1]
    wb_refs = refs[n_w + 2 * n_cast + 1:]

    @pl.when(pl.program_id(1) == 0)
    def _cast_weights():
        for w_ref, wb_ref in zip(w_refs, wb_refs):
            wb_ref[...] = w_ref[...].astype(BF16)

    for src_ref, dst_ref in zip(cast_in, cast_out):
        dst_ref[...] = src_ref[...].astype(BF16)

    hn = hn_ref[...]
    zs = [jnp.dot(hn, wb_ref[...], preferred_element_type=F32) for wb_ref in wb_refs]
    y = epilogue(*zs).astype(o_ref.dtype)
    for s in range(o_ref.shape[0]):
        o_ref[s] = y[:, s * LANES:(s + 1) * LANES]


def _inproj(hn, w_in, layer, col_starts, n_cols, epilogue, name, cast_weights=()):
    t, d = hn.shape
    n_w = len(col_starts)
    tn = COLS_INPROJ if n_w == 1 else COLS_GLU
    tm = ROWS_INPROJ
    assert t % tm == 0 and n_cols % tn == 0 and all(c % tn == 0 for c in col_starts)
    grid = (n_cols // tn, t // tm)

    def w_spec(c0):
        return pl.BlockSpec((None, d, tn), lambda j, i: (layer, 0, c0 // tn + j))

    n_cast = len(cast_weights)
    cast_in_specs, cast_out_specs, cast_shapes = [], [], []
    cast_bytes = 0
    if n_cast:
        rows, cols = cast_weights[0].shape[1:]
        assert all(w.shape[1:] == (rows, cols) for w in cast_weights)
        n_steps = grid[0] * grid[1]
        n_slabs = 1
        while n_slabs * 2 <= n_steps and rows % (n_slabs * 2 * 16) == 0:
            n_slabs *= 2
        slab_rows = rows // n_slabs

        def slab_index(j, i):
            return jnp.minimum(j * grid[1] + i, n_slabs - 1)

        cast_in_specs = [pl.BlockSpec((None, slab_rows, cols),
                                      lambda j, i: (layer, slab_index(j, i), 0))] * n_cast
        cast_out_specs = [pl.BlockSpec((slab_rows, cols),
                                       lambda j, i: (slab_index(j, i), 0))] * n_cast
        cast_shapes = [jax.ShapeDtypeStruct((rows, cols), BF16)] * n_cast
        cast_bytes = n_cast * 2 * slab_rows * cols * (4 + 2)

    vmem = (2 * tm * d * 2 + n_w * (2 * d * tn * 4 + d * tn * 2)
            + 2 * tm * tn * 2 + (n_w + 2) * tm * tn * 4 + cast_bytes + 4 * MIB)
    res = pl.pallas_call(
        functools.partial(_inproj_kernel, n_w=n_w, n_cast=n_cast, epilogue=epilogue),
        grid=grid,
        in_specs=[pl.BlockSpec((tm, d), lambda j, i: (i, 0))] + [w_spec(c) for c in col_starts]
                 + cast_in_specs,
        out_specs=[pl.BlockSpec((tn // LANES, tm, LANES), lambda j, i: (j, i, 0))]
                  + cast_out_specs,
        out_shape=[jax.ShapeDtypeStruct((n_cols // LANES, t, LANES), BF16)] + cast_shapes,
        scratch_shapes=[pltpu.VMEM((d, tn), BF16) for _ in range(n_w)],
        compiler_params=pltpu.CompilerParams(
            dimension_semantics=("parallel", "arbitrary"), vmem_limit_bytes=vmem),
        name=name,
    )(hn, *([w_in] * n_w), *cast_weights)
    return (res[0], *res[1:]) if n_cast else res[0]


def _left_halo(src_ref, halo, n_groups):
    tail = src_ref[pl.ds((n_groups - halo) * NSEG, halo * NSEG), :].astype(F32)
    rolled = pltpu.roll(tail, 1, 0).reshape(halo, NSEG, LANES)
    seg = lax.broadcasted_iota(jnp.int32, (halo, NSEG, LANES), 1)
    return jnp.where(seg == 0, 0.0, rolled).reshape(halo * NSEG, LANES)


def _zero_after(*arrays):
    words = []
    for a in arrays:
        bits = lax.bitcast_convert_type(a, jnp.uint32)
        words += [bits[i:i + 8] for i in range(0, a.shape[0], 8)]
    word = functools.reduce(jnp.bitwise_or, words)
    word = lax.shift_right_logical(lax.shift_right_logical(word, jnp.uint32(16)), jnp.uint32(16))
    return lax.bitcast_convert_type(word, F32)


def _order_after(lhs, anchors):
    if not anchors:
        return lhs
    zero = _zero_after(*anchors)
    zero = jnp.concatenate([zero, zero], axis=0).astype(lhs.dtype)
    head = lhs[:16] + jnp.tile(zero, (1, lhs.shape[1] // LANES))
    return jnp.concatenate([head, lhs[16:]], axis=0)


def _mixproj_kernel(hn_ref, wsa_ref, wsb_ref, wma_ref, wmb_ref, xa_ref, glu_ref,
                    caw_ref, cab_ref, wr_ref, wi_ref, br_ref, bi_ref, lam_ref, cbw_ref, cbb_ref,
                    ssb_ref, gma_ref, gmb_ref, ybc_ref, ya_ref,
                    wb_ref, xa_halo, xe_b, hloc, pcum, ssa_buf, h_state, p_state, cstart_ref):
    rt = pl.program_id(1)
    n_rt = pl.num_programs(1)
    t = xa_ref.shape[0]
    n_groups = t // NSEG
    k_a = caw_ref.shape[0]
    k_b = cbw_ref.shape[0]
    tile = MIX_TILE_GROUPS * NSEG
    r0 = pl.multiple_of(rt * tile, tile)

    @pl.when(rt == 0)
    def _head_start():
        for s, w_ref in enumerate((wsa_ref, wsb_ref, wma_ref, wmb_ref)):
            wb_ref[:, s * LANES:(s + 1) * LANES] = w_ref[...].astype(BF16)
        xe_b[pl.ds(0, (k_b - 1) * NSEG), :] = _left_halo(glu_ref, k_b - 1, n_groups)

    xe_b[pl.ds((k_b - 1) * NSEG + r0, tile), :] = glu_ref[pl.ds(r0, tile), :].astype(F32)

    w_gate = jnp.concatenate([wr_ref[...], wi_ref[...]], axis=1).astype(BF16)
    neg_lam = -lam_ref[...]
    softplus = jnp.maximum(neg_lam, 0.0) + jnp.log1p(jnp.exp(-jnp.abs(neg_lam)))
    c_lru = -LRU_C * softplus
    caw = caw_ref[...]
    halo_a = (k_a - 1) * NSEG
    xa_tile = xa_ref[pl.ds(r0, tile), :].astype(F32)
    xa_prev = jnp.where(rt == 0, _left_halo(xa_ref, k_a - 1, n_groups), xa_halo[...])
    xa_halo[...] = xa_tile[tile - halo_a:]
    xa_in = jnp.concatenate([xa_prev, xa_tile], axis=0)
    y = cab_ref[...] + caw[0:1] * xa_in[:tile]
    for k in range(1, k_a):
        y = y + caw[k:k + 1] * xa_in[k * NSEG:k * NSEG + tile]
    g = jnp.dot(y.astype(BF16), w_gate, preferred_element_type=F32)
    cbw = cbw_ref[...]
    w_rows = [jnp.broadcast_to(cbw[k:k + 1], (8, LANES)) for k in range(k_b)]
    bias_b = jnp.broadcast_to(cbb_ref[...], (8, LANES))
    chain_tail = [None] * MIX_CONV_CHAINS
    for jj in range(MIX_TILE_GROUPS):
        accs = []
        for s in range(NSEG // 8):
            lane = (jj * (NSEG // 8) + s) % MIX_CONV_CHAINS
            acc = bias_b if chain_tail[lane] is None else bias_b + _zero_after(chain_tail[lane])
            for k in range(k_b):
                acc = acc + w_rows[k] * xe_b[pl.ds(r0 + (jj + k) * NSEG + s * 8, 8), :]
            chain_tail[lane] = acc
            accs.append(acc)
        ybc_ref[jj * NSEG:(jj + 1) * NSEG, :] = jnp.concatenate(accs, axis=0).astype(BF16)


    h_all = jnp.where(rt == 0, 0.0, h_state[...])
    p_all = jnp.where(rt == 0, 1.0, p_state[...])
    sub = NSEG // MIX_GATE_SPLIT
    h = [h_all[q * sub:(q + 1) * sub] for q in range(MIX_GATE_SPLIT)]
    p = [p_all[q * sub:(q + 1) * sub] for q in range(MIX_GATE_SPLIT)]
    b_r = jnp.broadcast_to(br_ref[...], (8, LANES))
    for jj in range(MIX_TILE_GROUPS):
        for q in range(MIX_GATE_SPLIT):
            rows = slice(jj * NSEG + q * sub, jj * NSEG + (q + 1) * sub)
            r = _sigmoid(g[rows, :LANES] + jnp.tile(b_r, (sub // 8, 1)))
            i = _sigmoid(g[rows, LANES:] + bi_ref[...])
            log_a = c_lru * r
            a_j = jnp.exp(log_a)
            mult = jnp.sqrt(jnp.maximum(-jnp.tanh(log_a) * (a_j * a_j + 1.0), 0.0))
            h[q] = a_j * h[q] + mult * (i * y[rows])
            p[q] = a_j * p[q]
            hloc[pl.ds(r0 + jj * NSEG + q * sub, sub), :] = h[q]
            pcum[pl.ds(r0 + jj * NSEG + q * sub, sub), :] = p[q]
            b_r = br_ref[...] + _zero_after(h[q])
    h_state[...] = jnp.concatenate(h, axis=0)
    p_state[...] = jnp.concatenate(p, axis=0)

    part = tile // MIX_DOT_PARTS
    n_pieces = 2 * MIX_DOT_PARTS
    anchors = [[] for _ in range(n_pieces)]
    piece = 0
    for m0 in range(0, tile, part):
        for half, (acts, dsts) in enumerate((
                ((_silu, _silu), ((ssa_buf, r0 + m0), (ssb_ref, m0))),
                ((_sigmoid, _sigmoid), ((gma_ref, m0), (gmb_ref, m0))))):
            hn = _order_after(hn_ref[m0:m0 + part, :], anchors[piece])
            z = jnp.dot(hn, wb_ref[:, half * 2 * LANES:(half + 1) * 2 * LANES],
                        preferred_element_type=F32)
            for s, (dst, row) in enumerate(dsts):
                out = acts[s](z[:, s * LANES:(s + 1) * LANES])
                dst[pl.ds(row, part), :] = out.astype(BF16)
                if piece + 2 < n_pieces:
                    anchors[piece + 2].append(out)
            piece += 1

    @pl.when(rt == n_rt - 1)
    def _head_end():
        c = jnp.zeros((1, LANES), F32)
        for seg in range(NSEG):
            cstart_ref[pl.ds(seg, 1), :] = c
            c = h_state[pl.ds(seg, 1), :] + p_state[pl.ds(seg, 1), :] * c
        c_start = cstart_ref[...]

        def fix_step(j, carry):
            q0 = pl.multiple_of(j * NSEG, NSEG)
            hh = hloc[pl.ds(q0, NSEG), :] + pcum[pl.ds(q0, NSEG), :] * c_start
            ya_ref[pl.ds(q0, NSEG), :] = (hh * ssa_buf[pl.ds(q0, NSEG), :].astype(F32)).astype(BF16)
            return carry

        lax.fori_loop(0, n_groups, fix_step, 0, unroll=3 if n_groups % 3 == 0 else 1)


def _mixproj(hn, w_in, xa, glu, layer, gate_cols, conv_a_w, conv_a_b, w_rg, b_rg, w_ig, b_ig,
             lru_lambda, conv_b_w, conv_b_b):
    n_heads, t, _ = xa.shape
    d = hn.shape[1]
    k_a = conv_a_w.shape[1]
    k_b = conv_b_w.shape[1]
    n_groups = t // NSEG
    tile = MIX_TILE_GROUPS * NSEG
    assert t % NSEG == 0 and n_groups % MIX_TILE_GROUPS == 0 and n_groups >= k_b - 1
    assert all(c % LANES == 0 for c in gate_cols)

    def w_strip(c0):
        return pl.BlockSpec((None, d, LANES), lambda c, i: (layer, 0, c0 // LANES + c))

    slab = pl.BlockSpec((None, t, LANES), lambda c, i: (c, 0, 0))
    slab_tile = pl.BlockSpec((None, tile, LANES), lambda c, i: (c, i, 0))

    def per_channel(k):
        return pl.BlockSpec((None, k, LANES), lambda c, i: (layer, 0, c))

    gate_w = pl.BlockSpec((None, None, LANES, LANES), lambda c, i: (layer, c, 0, 0))
    slab_shape = jax.ShapeDtypeStruct(xa.shape, BF16)
    row3 = lambda p: p.reshape(p.shape[0], 1, p.shape[1])
    vmem = (2 * tile * d * 2 + 4 * 2 * d * LANES * 4 + d * 4 * LANES * 2
            + 2 * 2 * t * LANES * 2 + (3 * t + (k_a + k_b - 2) * NSEG) * LANES * 4
            + t * LANES * 2 + 2 * t * LANES * 2 + 4 * 2 * tile * LANES * 2
            + 10 * tile * LANES * 4 + 4 * MIB)
    return pl.pallas_call(
        _mixproj_kernel,
        grid=(n_heads, n_groups // MIX_TILE_GROUPS),
        in_specs=[pl.BlockSpec((tile, d), lambda c, i: (i, 0))]
                 + [w_strip(c0) for c0 in gate_cols]
                 + [slab, slab, per_channel(k_a), per_channel(1), gate_w, gate_w,
                    per_channel(1), per_channel(1), per_channel(1), per_channel(k_b),
                    per_channel(1)],
        out_specs=[slab_tile, slab_tile, slab_tile, slab_tile, slab],
        out_shape=[slab_shape] * 5,
        scratch_shapes=[
            pltpu.VMEM((d, 4 * LANES), BF16),
            pltpu.VMEM(((k_a - 1) * NSEG, LANES), F32),
            pltpu.VMEM(((k_b - 1 + n_groups) * NSEG, LANES), F32),
            pltpu.VMEM((t, LANES), F32),
            pltpu.VMEM((t, LANES), F32),
            pltpu.VMEM((t, LANES), BF16),
            pltpu.VMEM((NSEG, LANES), F32),
            pltpu.VMEM((NSEG, LANES), F32),
            pltpu.VMEM((NSEG, LANES), F32),
        ],
        compiler_params=pltpu.CompilerParams(
            dimension_semantics=("parallel", "arbitrary"), vmem_limit_bytes=vmem),
        name="mixproj",
    )(hn, w_in, w_in, w_in, w_in, xa, glu, conv_a_w, row3(conv_a_b), w_rg, w_ig, row3(b_rg),
      row3(b_ig), row3(lru_lambda), conv_b_w, row3(conv_b_b))


def _from_slabs(ref):
    return jnp.concatenate([ref[s] for s in range(ref.shape[0])], axis=1)


def _out_kernel(ya_ref, ybc_ref, ssb_ref, gma_ref, gmb_ref, h_ref, wpa_ref, wpb_ref, wout_ref,
                lng_ref, lnb_ref, gn_ref, *out_refs, last):
    yc = _from_slabs(ybc_ref).astype(F32)
    mu = jnp.mean(yc, axis=-1, keepdims=True)
    xc = yc - mu
    var = jnp.mean(xc * xc, axis=-1, keepdims=True)
    ln = xc * lax.rsqrt(var + EPS) * lng_ref[...] + lnb_ref[...]
    yb = (_silu(ln) * _from_slabs(ssb_ref).astype(F32)).astype(BF16)
    pa = jnp.dot(_from_slabs(ya_ref), wpa_ref[...], preferred_element_type=F32)
    pb = jnp.dot(yb, wpb_ref[...], preferred_element_type=F32)
    merged = (_from_slabs(gma_ref).astype(F32) * pa
              + _from_slabs(gmb_ref).astype(F32) * pb).astype(BF16)
    h_new = h_ref[...] + jnp.dot(merged, wout_ref[...], preferred_element_type=F32)
    ms = jnp.mean(h_new * h_new, axis=-1, keepdims=True)
    hn = h_new * lax.rsqrt(ms + EPS) * gn_ref[...]
    if last:
        out_refs[0][...] = hn
    else:
        out_refs[0][...] = h_new
        out_refs[1][...] = hn.astype(BF16)


def _out_block(ya, ybc, ssb, gma, gmb, h, wpa, wpb, wout, ln_g, ln_b, g_next, last):
    n_slabs, t, _ = ya.shape
    d = h.shape[1]
    tm = ROWS_OUT
    assert t % tm == 0
    slab = pl.BlockSpec((n_slabs, tm, LANES), lambda i: (0, i, 0))
    rows = pl.BlockSpec((tm, d), lambda i: (i, 0))
    weight = pl.BlockSpec((d, d), lambda i: (0, 0), pipeline_mode=pl.Buffered(1))
    vec = pl.BlockSpec((1, d), lambda i: (0, 0))
    if last:
        out_shape = [jax.ShapeDtypeStruct((t, d), F32)]
        out_specs = [rows]
    else:
        out_shape = [jax.ShapeDtypeStruct((t, d), F32), jax.ShapeDtypeStruct((t, d), BF16)]
        out_specs = [rows, rows]
    vmem = 3 * d * d * 2 + 2 * tm * d * (5 * 2 + 4 + 4 + 2) + 8 * tm * d * 4 + 4 * MIB
    return pl.pallas_call(
        functools.partial(_out_kernel, last=last),
        grid=(t // tm,),
        in_specs=[slab, slab, slab, slab, slab, rows, weight, weight, weight, vec, vec, vec],
        out_specs=out_specs,
        out_shape=out_shape,
        compiler_params=pltpu.CompilerParams(
            dimension_semantics=("parallel",), vmem_limit_bytes=vmem),
        name="out_block",
    )(ya, ybc, ssb, gma, gmb, h, wpa, wpb, wout, ln_g.reshape(1, d), ln_b.reshape(1, d),
      g_next.reshape(1, d))


def kernel(x, meta, norm_g, w_in, conv_a_w, conv_a_b, w_rg, b_rg, w_ig, b_ig, lru_lambda,
           conv_b_w, conv_b_b, ln_b_g, ln_b_b, w_proj_a, w_proj_b, w_out, final_g):
    bsz, seq, d = x.shape
    depth = w_in.shape[0]
    n_meta = meta.shape[0]
    t = n_meta + seq
    n_groups = t // NSEG
    assert t % NSEG == 0
    w_a = conv_a_w.shape[2]
    w_b = conv_b_w.shape[2]
    assert w_a == d and w_b == d
    c_sa, c_vb, c_gb, c_sb = w_a, 2 * w_a, 2 * w_a + w_b, 2 * w_a + 2 * w_b
    c_ma, c_mb = 2 * w_a + 3 * w_b, 2 * w_a + 3 * w_b + d

    outs = []
    for b in range(bsz):
        h = jnp.concatenate([meta.astype(x.dtype), x[b]], axis=0)
        h = h.reshape(NSEG, n_groups, d).transpose(1, 0, 2).reshape(t, d)
        hn = _rmsnorm(h, norm_g[0])
        for l in range(depth):
            xa = _inproj(hn, w_in, l, (0,), w_a, _identity, "inproj_xa")
            glu, wpa, wpb, wout = _inproj(hn, w_in, l, (c_vb, c_gb), w_b, _glu, "inproj_glu",
                                          cast_weights=(w_proj_a, w_proj_b, w_out))
            ssb, gma, gmb, ybc, ya = _mixproj(
                hn, w_in, xa, glu, l, (c_sa, c_sb, c_ma, c_mb), conv_a_w, conv_a_b,
                w_rg, b_rg, w_ig, b_ig, lru_lambda, conv_b_w, conv_b_b)
            last = l == depth - 1
            g_next = final_g if last else norm_g[l + 1]
            res = _out_block(ya, ybc, ssb, gma, gmb, h, wpa, wpb, wout, ln_b_g[l],
                             ln_b_b[l], g_next, last)
            if last:
                out = res[0]
            else:
                h, hn = res
        out = out.reshape(n_groups, NSEG, d).transpose(1, 0, 2).reshape(t, d)
        outs.append(out[n_meta:])
    return outs[0][None] if bsz == 1 else jnp.stack(outs, axis=0)
```

```python
import functools

import jax
import jax.numpy as jnp
from jax import lax
from jax.experimental import pallas as pl
from jax.experimental.pallas import tpu as pltpu

F32 = jnp.float32
BF16 = jnp.bfloat16

EPS = 1e-6
LRU_C = 8.0
LANES = 128
NSEG = 48

ROWS_INPROJ = 912
COLS_INPROJ = 1024
COLS_GLU = 512
ROWS_OUT = 304
ROWS_NORM = 432
MIX_TILE_GROUPS = 19
MIX_DOT_PARTS = 1
MIX_GATE_SPLIT = 3
MIX_CONV_CHAINS = 2

MIB = 1024 * 1024
VMEM_REQUEST_CAP = 60 * MIB


def _sigmoid(x):
    return 0.5 * jnp.tanh(0.5 * x) + 0.5


def _silu(x):
    return x * _sigmoid(x)


def _identity(x):
    return x


def _glu(v, g):
    return v * _sigmoid(g)


def _rmsnorm_kernel(x_ref, g_ref, o_ref):
    x = x_ref[...]
    ms = jnp.mean(x * x, axis=-1, keepdims=True)
    o_ref[...] = (x * lax.rsqrt(ms + EPS) * g_ref[...]).astype(o_ref.dtype)


def _rmsnorm(h, g):
    t, d = h.shape
    return pl.pallas_call(
        _rmsnorm_kernel,
        grid=(t // ROWS_NORM,),
        in_specs=[pl.BlockSpec((ROWS_NORM, d), lambda i: (i, 0)),
                  pl.BlockSpec((1, d), lambda i: (0, 0))],
        out_specs=pl.BlockSpec((ROWS_NORM, d), lambda i: (i, 0)),
        out_shape=jax.ShapeDtypeStruct((t, d), BF16),
        compiler_params=pltpu.CompilerParams(dimension_semantics=("parallel",)),
        name="rmsnorm0",
    )(h, g.reshape(1, d))


def _inproj_kernel(hn_ref, *refs, n_w, n_cast, epilogue):
    w_refs = refs[:n_w]
    cast_in = refs[n_w:n_w + n_cast]
    o_ref = refs[n_w + n_cast]
    cast_out = refs[n_w + n_cast + 1:n_w + 2 * n_cast + 1]
    wb_refs = refs[n_w + 2 * n_cast + 1:]

    @pl.when(pl.program_id(1) == 0)
    def _cast_weights():
        for w_ref, wb_ref in zip(w_refs, wb_refs):
            wb_ref[...] = w_ref[...].astype(BF16)

    for src_ref, dst_ref in zip(cast_in, cast_out):
        dst_ref[...] = src_ref[...].astype(BF16)

    hn = hn_ref[...]
    zs = [jnp.dot(hn, wb_ref[...], preferred_element_type=F32) for wb_ref in wb_refs]
    y = epilogue(*zs).astype(o_ref.dtype)
    for s in range(o_ref.shape[0]):
        o_ref[s] = y[:, s * LANES:(s + 1) * LANES]


def _inproj(hn, w_in, layer, col_starts, n_cols, epilogue, name, cast_weights=()):
    t, d = hn.shape
    n_w = len(col_starts)
    tn = COLS_INPROJ if n_w == 1 else COLS_GLU
    tm = ROWS_INPROJ
    assert t % tm == 0 and n_cols % tn == 0 and all(c % tn == 0 for c in col_starts)
    grid = (n_cols // tn, t // tm)

    def w_spec(c0):
        return pl.BlockSpec((None, d, tn), lambda j, i: (layer, 0, c0 // tn + j))

    n_cast = len(cast_weights)
    cast_in_specs, cast_out_specs, cast_shapes = [], [], []
    cast_bytes = 0
    if n_cast:
        rows, cols = cast_weights[0].shape[1:]
        assert all(w.shape[1:] == (rows, cols) for w in cast_weights)
        n_steps = grid[0] * grid[1]
        n_slabs = 1
        while n_slabs * 2 <= n_steps and rows % (n_slabs * 2 * 16) == 0:
            n_slabs *= 2
        slab_rows = rows // n_slabs

        def slab_index(j, i):
            return jnp.minimum(j * grid[1] + i, n_slabs - 1)

        cast_in_specs = [pl.BlockSpec((None, slab_rows, cols),
                                      lambda j, i: (layer, slab_index(j, i), 0))] * n_cast
        cast_out_specs = [pl.BlockSpec((slab_rows, cols),
                                       lambda j, i: (slab_index(j, i), 0))] * n_cast
        cast_shapes = [jax.ShapeDtypeStruct((rows, cols), BF16)] * n_cast
        cast_bytes = n_cast * 2 * slab_rows * cols * (4 + 2)

    vmem = (2 * tm * d * 2 + n_w * (2 * d * tn * 4 + d * tn * 2)
            + 2 * tm * tn * 2 + (n_w + 2) * tm * tn * 4 + cast_bytes + 4 * MIB)
    res = pl.pallas_call(
        functools.partial(_inproj_kernel, n_w=n_w, n_cast=n_cast, epilogue=epilogue),
        grid=grid,
        in_specs=[pl.BlockSpec((tm, d), lambda j, i: (i, 0))] + [w_spec(c) for c in col_starts]
                 + cast_in_specs,
        out_specs=[pl.BlockSpec((tn // LANES, tm, LANES), lambda j, i: (j, i, 0))]
                  + cast_out_specs,
        out_shape=[jax.ShapeDtypeStruct((n_cols // LANES, t, LANES), BF16)] + cast_shapes,
        scratch_shapes=[pltpu.VMEM((d, tn), BF16) for _ in range(n_w)],
        compiler_params=pltpu.CompilerParams(
            dimension_semantics=("parallel", "arbitrary"), vmem_limit_bytes=vmem),
        name=name,
    )(hn, *([w_in] * n_w), *cast_weights)
    return (res[0], *res[1:]) if n_cast else res[0]


def _left_halo(src_ref, halo, n_groups):
    tail = src_ref[pl.ds((n_groups - halo) * NSEG, halo * NSEG), :].astype(F32)
    rolled = pltpu.roll(tail, 1, 0).reshape(halo, NSEG, LANES)
    seg = lax.broadcasted_iota(jnp.int32, (halo, NSEG, LANES), 1)
    return jnp.where(seg == 0, 0.0, rolled).reshape(halo * NSEG, LANES)


def _zero_after(*arrays):
    words = []
    for a in arrays:
        bits = lax.bitcast_convert_type(a, jnp.uint32)
        words += [bits[i:i + 8] for i in range(0, a.shape[0], 8)]
    word = functools.reduce(jnp.bitwise_or, words)
    word = lax.shift_right_logical(lax.shift_right_logical(word, jnp.uint32(16)), jnp.uint32(16))
    return lax.bitcast_convert_type(word, F32)


def _order_after(lhs, anchors):
    if not anchors:
        return lhs
    zero = _zero_after(*anchors)
    zero = jnp.concatenate([zero, zero], axis=0).astype(lhs.dtype)
    head = lhs[:16] + jnp.tile(zero, (1, lhs.shape[1] // LANES))
    return jnp.concatenate([head, lhs[16:]], axis=0)


def _mixproj_kernel(hn_ref, wsa_ref, wsb_ref, wma_ref, wmb_ref, xa_ref, glu_ref,
                    caw_ref, cab_ref, wr_ref, wi_ref, br_ref, bi_ref, lam_ref, cbw_ref, cbb_ref,
                    ssb_ref, gma_ref, gmb_ref, ybc_ref, ya_ref,
                    wb_ref, xa_halo, xe_b, hloc, pcum, ssa_buf, h_state, p_state, cstart_ref):
    rt = pl.program_id(1)
    n_rt = pl.num_programs(1)
    t = xa_ref.shape[0]
    n_groups = t // NSEG
    k_a = caw_ref.shape[0]
    k_b = cbw_ref.shape[0]
    tile = MIX_TILE_GROUPS * NSEG
    r0 = pl.multiple_of(rt * tile, tile)

    @pl.when(rt == 0)
    def _head_start():
        for s, w_ref in enumerate((wsa_ref, wsb_ref, wma_ref, wmb_ref)):
            wb_ref[:, s * LANES:(s + 1) * LANES] = w_ref[...].astype(BF16)
        xe_b[pl.ds(0, (k_b - 1) * NSEG), :] = _left_halo(glu_ref, k_b - 1, n_groups)

    xe_b[pl.ds((k_b - 1) * NSEG + r0, tile), :] = glu_ref[pl.ds(r0, tile), :].astype(F32)

    w_gate = jnp.concatenate([wr_ref[...], wi_ref[...]], axis=1).astype(BF16)
    neg_lam = -lam_ref[...]
    softplus = jnp.maximum(neg_lam, 0.0) + jnp.log1p(jnp.exp(-jnp.abs(neg_lam)))
    c_lru = -LRU_C * softplus
    caw = caw_ref[...]
    halo_a = (k_a - 1) * NSEG
    xa_tile = xa_ref[pl.ds(r0, tile), :].astype(F32)
    xa_prev = jnp.where(rt == 0, _left_halo(xa_ref, k_a - 1, n_groups), xa_halo[...])
    xa_halo[...] = xa_tile[tile - halo_a:]
    xa_in = jnp.concatenate([xa_prev, xa_tile], axis=0)
    y = cab_ref[...] + caw[0:1] * xa_in[:tile]
    for k in range(1, k_a):
        y = y + caw[k:k + 1] * xa_in[k * NSEG:k * NSEG + tile]
    g = jnp.dot(y.astype(BF16), w_gate, preferred_element_type=F32)
    cbw = cbw_ref[...]
    w_rows = [jnp.broadcast_to(cbw[k:k + 1], (8, LANES)) for k in range(k_b)]
    bias_b = jnp.broadcast_to(cbb_ref[...], (8, LANES))
    chain_tail = [None] * MIX_CONV_CHAINS
    for jj in range(MIX_TILE_GROUPS):
        accs = []
        for s in range(NSEG // 8):
            lane = (jj * (NSEG // 8) + s) % MIX_CONV_CHAINS
            acc = bias_b if chain_tail[lane] is None else bias_b + _zero_after(chain_tail[lane])
            for k in range(k_b):
                acc = acc + w_rows[k] * xe_b[pl.ds(r0 + (jj + k) * NSEG + s * 8, 8), :]
            chain_tail[lane] = acc
            accs.append(acc)
        ybc_ref[jj * NSEG:(jj + 1) * NSEG, :] = jnp.concatenate(accs, axis=0).astype(BF16)


    h_all = jnp.where(rt == 0, 0.0, h_state[...])
    p_all = jnp.where(rt == 0, 1.0, p_state[...])
    sub = NSEG // MIX_GATE_SPLIT
    h = [h_all[q * sub:(q + 1) * sub] for q in range(MIX_GATE_SPLIT)]
    p = [p_all[q * sub:(q + 1) * sub] for q in range(MIX_GATE_SPLIT)]
    b_r = jnp.broadcast_to(br_ref[...], (8, LANES))
    for jj in range(MIX_TILE_GROUPS):
        for q in range(MIX_GATE_SPLIT):
            rows = slice(jj * NSEG + q * sub, jj * NSEG + (q + 1) * sub)
            r = _sigmoid(g[rows, :LANES] + jnp.tile(b_r, (sub // 8, 1)))
            i = _sigmoid(g[rows, LANES:] + bi_ref[...])
            log_a = c_lru * r
            a_j = jnp.exp(log_a)
            mult = jnp.sqrt(jnp.maximum(-jnp.tanh(log_a) * (a_j * a_j + 1.0), 0.0))
            h[q] = a_j * h[q] + mult * (i * y[rows])
            p[q] = a_j * p[q]
            hloc[pl.ds(r0 + jj * NSEG + q * sub, sub), :] = h[q]
            pcum[pl.ds(r0 + jj * NSEG + q * sub, sub), :] = p[q]
            b_r = br_ref[...] + _zero_after(h[q])
    h_state[...] = jnp.concatenate(h, axis=0)
    p_state[...] = jnp.concatenate(p, axis=0)

    part = tile // MIX_DOT_PARTS
    n_pieces = 2 * MIX_DOT_PARTS
    anchors = [[] for _ in range(n_pieces)]
    piece = 0
    for m0 in range(0, tile, part):
        for half, (acts, dsts) in enumerate((
                ((_silu, _silu), ((ssa_buf, r0 + m0), (ssb_ref, m0))),
                ((_sigmoid, _sigmoid), ((gma_ref, m0), (gmb_ref, m0))))):
            hn = _order_after(hn_ref[m0:m0 + part, :], anchors[piece])
            z = jnp.dot(hn, wb_ref[:, half * 2 * LANES:(half + 1) * 2 * LANES],
                        preferred_element_type=F32)
            for s, (dst, row) in enumerate(dsts):
                out = acts[s](z[:, s * LANES:(s + 1) * LANES])
                dst[pl.ds(row, part), :] = out.astype(BF16)
                if piece + 2 < n_pieces:
                    anchors[piece + 2].append(out)
            piece += 1

    @pl.when(rt == n_rt - 1)
    def _head_end():
        c = jnp.zeros((1, LANES), F32)
        for seg in range(NSEG):
            cstart_ref[pl.ds(seg, 1), :] = c
            c = h_state[pl.ds(seg, 1), :] + p_state[pl.ds(seg, 1), :] * c
        c_start = cstart_ref[...]

        def fix_step(j, carry):
            q0 = pl.multiple_of(j * NSEG, NSEG)
            hh = hloc[pl.ds(q0, NSEG), :] + pcum[pl.ds(q0, NSEG), :] * c_start
            ya_ref[pl.ds(q0, NSEG), :] = (hh * ssa_buf[pl.ds(q0, NSEG), :].astype(F32)).astype(BF16)
            return carry

        lax.fori_loop(0, n_groups, fix_step, 0, unroll=3 if n_groups % 3 == 0 else 1)


def _mixproj(hn, w_in, xa, glu, layer, gate_cols, conv_a_w, conv_a_b, w_rg, b_rg, w_ig, b_ig,
             lru_lambda, conv_b_w, conv_b_b):
    n_heads, t, _ = xa.shape
    d = hn.shape[1]
    k_a = conv_a_w.shape[1]
    k_b = conv_b_w.shape[1]
    n_groups = t // NSEG
    tile = MIX_TILE_GROUPS * NSEG
    assert t % NSEG == 0 and n_groups % MIX_TILE_GROUPS == 0 and n_groups >= k_b - 1
    assert all(c % LANES == 0 for c in gate_cols)

    def w_strip(c0):
        return pl.BlockSpec((None, d, LANES), lambda c, i: (layer, 0, c0 // LANES + c))

    slab = pl.BlockSpec((None, t, LANES), lambda c, i: (c, 0, 0))
    slab_tile = pl.BlockSpec((None, tile, LANES), lambda c, i: (c, i, 0))

    def per_channel(k):
        return pl.BlockSpec((None, k, LANES), lambda c, i: (layer, 0, c))

    gate_w = pl.BlockSpec((None, None, LANES, LANES), lambda c, i: (layer, c, 0, 0))
    slab_shape = jax.ShapeDtypeStruct(xa.shape, BF16)
    row3 = lambda p: p.reshape(p.shape[0], 1, p.shape[1])
    vmem = (2 * tile * d * 2 + 4 * 2 * d * LANES * 4 + d * 4 * LANES * 2
            + 2 * 2 * t * LANES * 2 + (3 * t + (k_a + k_b - 2) * NSEG) * LANES * 4
            + t * LANES * 2 + 2 * t * LANES * 2 + 4 * 2 * tile * LANES * 2
            + 10 * tile * LANES * 4 + 4 * MIB)
    return pl.pallas_call(
        _mixproj_kernel,
        grid=(n_heads, n_groups // MIX_TILE_GROUPS),
        in_specs=[pl.BlockSpec((tile, d), lambda c, i: (i, 0))]
                 + [w_strip(c0) for c0 in gate_cols]
                 + [slab, slab, per_channel(k_a), per_channel(1), gate_w, gate_w,
                    per_channel(1), per_channel(1), per_channel(1), per_channel(k_b),
                    per_channel(1)],
        out_specs=[slab_tile, slab_tile, slab_tile, slab_tile, slab],
        out_shape=[slab_shape] * 5,
        scratch_shapes=[
            pltpu.VMEM((d, 4 * LANES), BF16),
            pltpu.VMEM(((k_a - 1) * NSEG, LANES), F32),
            pltpu.VMEM(((k_b - 1 + n_groups) * NSEG, LANES), F32),
            pltpu.VMEM((t, LANES), F32),
            pltpu.VMEM((t, LANES), F32),
            pltpu.VMEM((t, LANES), BF16),
            pltpu.VMEM((NSEG, LANES), F32),
            pltpu.VMEM((NSEG, LANES), F32),
            pltpu.VMEM((NSEG, LANES), F32),
        ],
        compiler_params=pltpu.CompilerParams(
            dimension_semantics=("parallel", "arbitrary"), vmem_limit_bytes=vmem),
        name="mixproj",
    )(hn, w_in, w_in, w_in, w_in, xa, glu, conv_a_w, row3(conv_a_b), w_rg, w_ig, row3(b_rg),
      row3(b_ig), row3(lru_lambda), conv_b_w, row3(conv_b_b))


def _from_slabs(ref):
    return jnp.concatenate([ref[s] for s in range(ref.shape[0])], axis=1)


def _out_kernel(ya_ref, ybc_ref, ssb_ref, gma_ref, gmb_ref, h_ref, wpa_ref, wpb_ref, wout_ref,
                lng_ref, lnb_ref, gn_ref, *out_refs, last):
    yc = _from_slabs(ybc_ref).astype(F32)
    mu = jnp.mean(yc, axis=-1, keepdims=True)
    xc = yc - mu
    var = jnp.mean(xc * xc, axis=-1, keepdims=True)
    ln = xc * lax.rsqrt(var + EPS) * lng_ref[...] + lnb_ref[...]
    yb = (_silu(ln) * _from_slabs(ssb_ref).astype(F32)).astype(BF16)
    pa = jnp.dot(_from_slabs(ya_ref), wpa_ref[...], preferred_element_type=F32)
    pb = jnp.dot(yb, wpb_ref[...], preferred_element_type=F32)
    merged = (_from_slabs(gma_ref).astype(F32) * pa
              + _from_slabs(gmb_ref).astype(F32) * pb).astype(BF16)
    h_new = h_ref[...] + jnp.dot(merged, wout_ref[...], preferred_element_type=F32)
    ms = jnp.mean(h_new * h_new, axis=-1, keepdims=True)
    hn = h_new * lax.rsqrt(ms + EPS) * gn_ref[...]
    if last:
        out_refs[0][...] = hn
    else:
        out_refs[0][...] = h_new
        out_refs[1][...] = hn.astype(BF16)


def _out_block(ya, ybc, ssb, gma, gmb, h, wpa, wpb, wout, ln_g, ln_b, g_next, last):
    n_slabs, t, _ = ya.shape
    d = h.shape[1]
    tm = ROWS_OUT
    assert t % tm == 0
    slab = pl.BlockSpec((n_slabs, tm, LANES), lambda i: (0, i, 0))
    rows = pl.BlockSpec((tm, d), lambda i: (i, 0))
    weight = pl.BlockSpec((d, d), lambda i: (0, 0), pipeline_mode=pl.Buffered(1))
    vec = pl.BlockSpec((1, d), lambda i: (0, 0))
    if last:
        out_shape = [jax.ShapeDtypeStruct((t, d), F32)]
        out_specs = [rows]
    else:
        out_shape = [jax.ShapeDtypeStruct((t, d), F32), jax.ShapeDtypeStruct((t, d), BF16)]
        out_specs = [rows, rows]
    vmem = min(3 * d * d * 2 + 2 * tm * d * (5 * 2 + 4 + 4 + 2) + 8 * tm * d * 4 + 4 * MIB,
               VMEM_REQUEST_CAP)
    return pl.pallas_call(
        functools.partial(_out_kernel, last=last),
        grid=(t // tm,),
        in_specs=[slab, slab, slab, slab, slab, rows, weight, weight, weight, vec, vec, vec],
        out_specs=out_specs,
        out_shape=out_shape,
        compiler_params=pltpu.CompilerParams(
            dimension_semantics=("parallel",), vmem_limit_bytes=vmem),
        name="out_block",
    )(ya, ybc, ssb, gma, gmb, h, wpa, wpb, wout, ln_g.reshape(1, d), ln_b.reshape(1, d),
      g_next.reshape(1, d))


def kernel(x, meta, norm_g, w_in, conv_a_w, conv_a_b, w_rg, b_rg, w_ig, b_ig, lru_lambda,
           conv_b_w, conv_b_b, ln_b_g, ln_b_b, w_proj_a, w_proj_b, w_out, final_g):
    bsz, seq, d = x.shape
    depth = w_in.shape[0]
    n_meta = meta.shape[0]
    t = n_meta + seq
    n_groups = t // NSEG
    assert t % NSEG == 0
    w_a = conv_a_w.shape[2]
    w_b = conv_b_w.shape[2]
    assert w_a == d and w_b == d
    c_sa, c_vb, c_gb, c_sb = w_a, 2 * w_a, 2 * w_a + w_b, 2 * w_a + 2 * w_b
    c_ma, c_mb = 2 * w_a + 3 * w_b, 2 * w_a + 3 * w_b + d

    outs = []
    for b in range(bsz):
        h = jnp.concatenate([meta.astype(x.dtype), x[b]], axis=0)
        h = h.reshape(NSEG, n_groups, d).transpose(1, 0, 2).reshape(t, d)
        hn = _rmsnorm(h, norm_g[0])
        for l in range(depth):
            xa = _inproj(hn, w_in, l, (0,), w_a, _identity, "inproj_xa")
            glu, wpa, wpb, wout = _inproj(hn, w_in, l, (c_vb, c_gb), w_b, _glu, "inproj_glu",
                                          cast_weights=(w_proj_a, w_proj_b, w_out))
            ssb, gma, gmb, ybc, ya = _mixproj(
                hn, w_in, xa, glu, l, (c_sa, c_sb, c_ma, c_mb), conv_a_w, conv_a_b,
                w_rg, b_rg, w_ig, b_ig, lru_lambda, conv_b_w, conv_b_b)
            last = l == depth - 1
            g_next = final_g if last else norm_g[l + 1]
            res = _out_block(ya, ybc, ssb, gma, gmb, h, wpa, wpb, wout, ln_b_g[l],
                             ln_b_b[l], g_next, last)
            if last:
                out = res[0]
            else:
                h, hn = res
        out = out.reshape(n_groups, NSEG, d).transpose(1, 0, 2).reshape(t, d)
        outs.append(out[n_meta:])
    return outs[0][None] if bsz == 1 else jnp.stack(outs, axis=0)
```
